```python
import jax, jax.numpy as jnp
from jax import lax
import numpy as np

D_MODEL = 1024
BATCH = 16
SEQ = 2048
DEPTH = 2

MIX_WIDTH = D_MODEL
GM_HEADS = 4
GM_WIDTH = MIX_WIDTH // 2
GM_HEAD_DIM = GM_WIDTH // GM_HEADS
GM_CHUNK = 128
GLA_HEADS = 4
GLA_WIDTH = MIX_WIDTH - GM_WIDTH
GLA_DV = GLA_WIDTH // GLA_HEADS
GLA_KEY_WIDTH = GLA_WIDTH // 2
GLA_DK = GLA_KEY_WIDTH // GLA_HEADS
GLA_GATE_RANK = 16
GLA_TAU = 16.0
GLA_CHUNK = 64
IN_COLS = 2 * GM_WIDTH + 2 * GLA_KEY_WIDTH + 2 * GLA_WIDTH + GLA_GATE_RANK
D_FF = 2816
N_EXPERTS = 8
TOP_K = 2
D_FF_EXPERT = 3584
N_DENSE = (DEPTH + 1) // 2
N_MOE = DEPTH // 2
LN_EPS = 1e-5
RMS_EPS = 1e-6
DEEPNORM_ALPHA = (2 * DEPTH) ** 0.25
DEEPNORM_BETA = (8 * DEPTH) ** -0.25

kernel_name = 'hybrid_gmlp_gla_deepnorm_moe'


def layer_norm(x, g, b):
    xf = x.astype(jnp.float32)
    mu = jnp.mean(xf, axis=-1, keepdims=True)
    var = jnp.mean(jnp.square(xf - mu), axis=-1, keepdims=True)
    y = (xf - mu) * lax.rsqrt(var + LN_EPS) * g.astype(jnp.float32) + b.astype(jnp.float32)
    return y.astype(x.dtype)


def chunked_spatial_gating(u, v, w_s, b_s, ln_g, ln_b):
    B, S, _ = u.shape
    n = S // GM_CHUNK
    v = layer_norm(v.reshape(B, S, GM_HEADS, GM_HEAD_DIM),
                   ln_g.reshape(GM_HEADS, GM_HEAD_DIM), ln_b.reshape(GM_HEADS, GM_HEAD_DIM))
    v = v.reshape(B, n, GM_CHUNK, GM_HEADS, GM_HEAD_DIM)
    causal = jnp.tril(jnp.ones((GM_CHUNK, GM_CHUNK), dtype=bool))
    w = jnp.where(causal[None], w_s, 0)
    s = jnp.einsum('hts,bnshc->bnthc', w, v) + b_s.T[None, None, :, :, None]
    return u * s.reshape(B, S, GM_WIDTH)


def gla_chunked(q, k, v, log_a):
    B, S, H, dk = q.shape
    dv = v.shape[-1]
    n = S // GLA_CHUNK

    def to_chunks(t):
        return t.astype(jnp.float32).reshape(B, n, GLA_CHUNK, H, t.shape[-1]).transpose(1, 0, 3, 2, 4)

    q, k, v, log_a = to_chunks(q), to_chunks(k), to_chunks(v), to_chunks(log_a)
    q = q * (dk ** -0.5)
    b = jnp.cumsum(log_a, axis=3)
    b_last = b[:, :, :, -1:, :]
    q_dec = q * jnp.exp(b)
    k_inv = k * jnp.exp(-b)
    k_dec = k * jnp.exp(b_last - b)
    causal = jnp.tril(jnp.ones((GLA_CHUNK, GLA_CHUNK), dtype=bool))
    scores = jnp.where(causal, jnp.einsum('nbhid,nbhjd->nbhij', q_dec, k_inv), 0.0)
    o_intra = jnp.einsum('nbhij,nbhje->nbhie', scores, v)

    def step(state, inp):
        q_c, k_c, v_c, decay_c = inp
        o = jnp.einsum('bhid,bhde->bhie', q_c, state)
        state = state * decay_c[..., 0, :, None] + jnp.einsum('bhjd,bhje->bhde', k_c, v_c)
        return state, o

    state0 = jnp.zeros((B, H, dk, dv), jnp.float32)
    _, o_inter = lax.scan(step, state0, (q_dec, k_dec, v, jnp.exp(b_last)))
    o = o_intra + o_inter
    return o.transpose(1, 0, 3, 2, 4).reshape(B, S, H, dv)


def hybrid_mixer(x, w_in, gm_ws, gm_bs, gm_ln_g, gm_ln_b, gla_wa2, gla_ba, gla_norm_g, w_out):
    B, S, _ = x.shape
    proj = x @ w_in
    o1 = GM_WIDTH
    o2 = o1 + GM_WIDTH
    o3 = o2 + GLA_KEY_WIDTH
    o4 = o3 + GLA_KEY_WIDTH
    o5 = o4 + GLA_WIDTH
    o6 = o5 + GLA_WIDTH
    gm_u, gm_v, q, k, v, g, a_lr = jnp.split(proj, [o1, o2, o3, o4, o5, o6], axis=-1)
    gm_u = jax.nn.gelu(gm_u, approximate=False)
    gm_v = jax.nn.gelu(gm_v, approximate=False)
    out_a = chunked_spatial_gating(gm_u, gm_v, gm_ws, gm_bs, gm_ln_g, gm_ln_b)
    log_a = jax.nn.log_sigmoid((a_lr @ gla_wa2 + gla_ba).astype(jnp.float32)) / GLA_TAU
    o = gla_chunked(q.reshape(B, S, GLA_HEADS, GLA_DK), k.reshape(B, S, GLA_HEADS, GLA_DK),
                    v.reshape(B, S, GLA_HEADS, GLA_DV), log_a.reshape(B, S, GLA_HEADS, GLA_DK))
    o = o * lax.rsqrt(jnp.mean(jnp.square(o), axis=-1, keepdims=True) + RMS_EPS)
    o = o * gla_norm_g.reshape(GLA_HEADS, GLA_DV).astype(jnp.float32)
    out_b = o.reshape(B, S, GLA_WIDTH).astype(x.dtype) * jax.nn.silu(g)
    return jnp.concatenate([out_a, out_b], axis=-1) @ w_out


def swiglu(x, w_gate, w_up, w_down):
    return (jax.nn.silu(x @ w_gate) * (x @ w_up)) @ w_down


def moe_swiglu(x, w_router, e_gate, e_up, e_down):
    B, S, D = x.shape
    t = x.reshape(B * S, D)
    logits = (t @ w_router).astype(jnp.float32)
    top_val, top_idx = lax.top_k(logits, TOP_K)
    top_w = jax.nn.softmax(top_val, axis=-1)
    gates = jnp.sum(jax.nn.one_hot(top_idx, N_EXPERTS, dtype=jnp.float32) * top_w[..., None], axis=1)
    gates = gates.astype(t.dtype)
    out = jnp.zeros_like(t)
    for e in range(N_EXPERTS):
        out = out + gates[:, e:e + 1] * swiglu(t, e_gate[e], e_up[e], e_down[e])
    return out.reshape(B, S, D)


def setup_inputs(seed: int = 0) -> dict:
    key = jax.random.key(seed)
    ks = jax.random.split(key, 22)
    nrm = jax.random.normal
    f32 = jnp.float32
    D = D_MODEL
    return {
        'x': nrm(ks[0], (BATCH, SEQ, D), f32),
        'w_in': nrm(ks[1], (DEPTH, D, IN_COLS), f32) * D ** -0.5,
        'gm_ws': nrm(ks[2], (DEPTH, GM_HEADS, GM_CHUNK, GM_CHUNK), f32) * GM_CHUNK ** -0.5,
        'gm_bs': 1.0 + 0.1 * nrm(ks[3], (DEPTH, GM_HEADS, GM_CHUNK), f32),
        'gm_ln_g': 1.0 + 0.1 * nrm(ks[4], (DEPTH, GM_WIDTH), f32),
        'gm_ln_b': 0.02 * nrm(ks[5], (DEPTH, GM_WIDTH), f32),
        'gla_wa2': nrm(ks[6], (DEPTH, GLA_GATE_RANK, GLA_KEY_WIDTH), f32) * GLA_GATE_RANK ** -0.5,
        'gla_ba': 0.1 * nrm(ks[7], (DEPTH, GLA_KEY_WIDTH), f32),
        'gla_norm_g': 1.0 + 0.1 * nrm(ks[8], (DEPTH, GLA_WIDTH), f32),
        'w_out': nrm(ks[9], (DEPTH, MIX_WIDTH, D), f32) * MIX_WIDTH ** -0.5 * DEEPNORM_BETA,
        'ln_mix_g': 1.0 + 0.1 * nrm(ks[10], (DEPTH, D), f32),
        'ln_mix_b': 0.02 * nrm(ks[11], (DEPTH, D), f32),
        'ffn_w_gate': nrm(ks[12], (N_DENSE, D, D_FF), f32) * D ** -0.5,
        'ffn_w_up': nrm(ks[13], (N_DENSE, D, D_FF), f32) * D ** -0.5,
        'ffn_w_down': nrm(ks[14], (N_DENSE, D_FF, D), f32) * D_FF ** -0.5 * DEEPNORM_BETA,
        'router_w': nrm(ks[15], (N_MOE, D, N_EXPERTS), f32) * D ** -0.5,
        'exp_w_gate': nrm(ks[16], (N_MOE, N_EXPERTS, D, D_FF_EXPERT), f32) * D ** -0.5,
        'exp_w_up': nrm(ks[17], (N_MOE, N_EXPERTS, D, D_FF_EXPERT), f32) * D ** -0.5,
        'exp_w_down': nrm(ks[18], (N_MOE, N_EXPERTS, D_FF_EXPERT, D), f32) * D_FF_EXPERT ** -0.5 * DEEPNORM_BETA,
        'ln_ffn_g': 1.0 + 0.1 * nrm(ks[19], (DEPTH, D), f32),
        'ln_ffn_b': 0.02 * nrm(ks[20], (DEPTH, D), f32),
    }


def reference(x, w_in, gm_ws, gm_bs, gm_ln_g, gm_ln_b, gla_wa2, gla_ba, gla_norm_g, w_out,
              ln_mix_g, ln_mix_b, ffn_w_gate, ffn_w_up, ffn_w_down, router_w,
              exp_w_gate, exp_w_up, exp_w_down, ln_ffn_g, ln_ffn_b):
    for layer in range(DEPTH):
        h = hybrid_mixer(x, w_in[layer], gm_ws[layer], gm_bs[layer], gm_ln_g[layer], gm_ln_b[layer],
                         gla_wa2[layer], gla_ba[layer], gla_norm_g[layer], w_out[layer])
        x = layer_norm(DEEPNORM_ALPHA * x + h, ln_mix_g[layer], ln_mix_b[layer])
        i = layer // 2
        if layer % 2 == 0:
            f = swiglu(x, ffn_w_gate[i], ffn_w_up[i], ffn_w_down[i])
        else:
            f = moe_swiglu(x, router_w[i], exp_w_gate[i], exp_w_up[i], exp_w_down[i])
        x = layer_norm(DEEPNORM_ALPHA * x + f, ln_ffn_g[layer], ln_ffn_b[layer])
    return x
```

```python
import functools

import jax
import jax.numpy as jnp
from jax import lax
from jax.experimental import pallas as pl
from jax.experimental.pallas import tpu as pltpu

F32 = jnp.float32
BF16 = jnp.bfloat16

D_MODEL = 1024
DEPTH = 2
GM_HEADS = 4
GM_WIDTH = 512
GM_HEAD_DIM = 128
GM_CHUNK = 128
GLA_HEADS = 4
GLA_WIDTH = 512
GLA_DV = 128
GLA_KEY_WIDTH = 256
GLA_DK = 64
GLA_GATE_RANK = 16
GLA_TAU = 16.0
GLA_CHUNK = 64
PROJ_MAIN = 2 * GM_WIDTH + 2 * GLA_KEY_WIDTH + 2 * GLA_WIDTH
N_EXPERTS = 8
TOP_K = 2
LN_EPS = 1e-5
RMS_EPS = 1e-6
DEEPNORM_ALPHA = (2 * DEPTH) ** 0.25

V7X_VMEM_LIMIT_BYTES = 56 * 1024 * 1024
LANE = 128
SUBLANE = 8
MXU_N = 256

ROW_TILE = 512
FFN_ROW_TILE = 1024
DENSE_FF_TILE = 256
MOE_FF_TILE = 512
ROUTER_TILE = 512
MOVE_TILE = 1024


def _params(*sem):
    return pltpu.CompilerParams(dimension_semantics=sem, vmem_limit_bytes=V7X_VMEM_LIMIT_BYTES)


def _gelu(x):
    return 0.5 * x * (1.0 + lax.erf(x * (2.0 ** -0.5)))


def _layer_norm(y, g, b):
    mu = jnp.mean(y, axis=-1, keepdims=True)
    d = y - mu
    var = jnp.mean(d * d, axis=-1, keepdims=True)
    return d * lax.rsqrt(var + LN_EPS) * g + b


def _fold_kernel(w1_ref, w2_ref, o_ref):
    acc = jnp.zeros(o_ref.shape, F32)
    for r in range(GLA_GATE_RANK):
        acc = acc + w1_ref[:, r:r + 1] * w2_ref[r:r + 1, :]
    o_ref[...] = acc


def _fold_gate(w1, w2):
    return pl.pallas_call(
        _fold_kernel,
        out_shape=jax.ShapeDtypeStruct((w1.shape[0], w2.shape[1]), F32),
        name="fold_gate",
    )(w1, w2)


def _inproj_kernel(x_ref, w_ref, p_ref, z_ref):
    xb = x_ref[...].astype(BF16)
    for n in range(PROJ_MAIN // MXU_N):
        cols = slice(n * MXU_N, (n + 1) * MXU_N)
        p_ref[:, cols] = jnp.dot(xb, w_ref[:, cols], preferred_element_type=F32).astype(BF16)
    z_ref[...] = jnp.dot(xb, w_ref[:, PROJ_MAIN:], preferred_element_type=F32)


def _inproj(x, w):
    t, d = x.shape
    n = w.shape[1]
    return pl.pallas_call(
        _inproj_kernel,
        grid=(t // ROW_TILE,),
        in_specs=[pl.BlockSpec((ROW_TILE, d), lambda i: (i, 0)),
                  pl.BlockSpec((d, n), lambda i: (0, 0))],
        out_specs=[pl.BlockSpec((ROW_TILE, PROJ_MAIN), lambda i: (i, 0)),
                   pl.BlockSpec((ROW_TILE, GLA_KEY_WIDTH), lambda i: (i, 0))],
        out_shape=[jax.ShapeDtypeStruct((t, PROJ_MAIN), BF16),
                   jax.ShapeDtypeStruct((t, GLA_KEY_WIDTH), F32)],
        compiler_params=_params("parallel"),
        name="inproj",
    )(x, w)


def _gmlp_kernel(uv_ref, w_ref, bcol_ref, g_ref, b_ref, o_ref):
    rows = uv_ref.shape[0]
    r = lax.broadcasted_iota(jnp.int32, (GM_CHUNK, GM_CHUNK), 0)
    c = lax.broadcasted_iota(jnp.int32, (GM_CHUNK, GM_CHUNK), 1)
    causal = r >= c
    for h in range(GM_HEADS):
        cols = slice(h * GM_HEAD_DIM, (h + 1) * GM_HEAD_DIM)
        vcols = slice(GM_WIDTH + h * GM_HEAD_DIM, GM_WIDTH + (h + 1) * GM_HEAD_DIM)
        wm = jnp.where(causal, w_ref[h], 0.0).astype(BF16)
        v = _gelu(uv_ref[:, vcols].astype(F32))
        vn = _layer_norm(v, g_ref[:, cols], b_ref[:, cols]).astype(BF16)
        u = _gelu(uv_ref[:, cols].astype(F32))
        for ci in range(rows // GM_CHUNK):
            rs = slice(ci * GM_CHUNK, (ci + 1) * GM_CHUNK)
            s = jnp.dot(wm, vn[rs], preferred_element_type=F32) + bcol_ref[h]
            o_ref[rs, cols] = (u[rs] * s).astype(BF16)


def _gmlp(proj, w_s, b_col, ln_g, ln_b):
    t = proj.shape[0]
    return pl.pallas_call(
        _gmlp_kernel,
        grid=(t // ROW_TILE,),
        in_specs=[pl.BlockSpec((ROW_TILE, 2 * GM_WIDTH), lambda i: (i, 0)),
                  pl.BlockSpec((GM_HEADS, GM_CHUNK, GM_CHUNK), lambda i: (0, 0, 0)),
                  pl.BlockSpec((GM_HEADS, GM_CHUNK, 1), lambda i: (0, 0, 0)),
                  pl.BlockSpec((1, GM_WIDTH), lambda i: (0, 0)),
                  pl.BlockSpec((1, GM_WIDTH), lambda i: (0, 0))],
        out_specs=pl.BlockSpec((ROW_TILE, GM_WIDTH), lambda i: (i, 0)),
        out_shape=jax.ShapeDtypeStruct((t, GM_WIDTH), BF16),
        compiler_params=_params("parallel"),
        name="gmlp",
    )(proj, w_s, b_col, ln_g, ln_b)


def _gla_kernel(q_ref, k_ref, v_ref, g_ref, z_ref, ba_ref, ng_ref, o_ref, st_ref):
    rows = q_ref.shape[0]

    @pl.when(pl.program_id(1) == 0)
    def _():
        st_ref[...] = jnp.zeros(st_ref.shape, F32)

    r = lax.broadcasted_iota(jnp.int32, (GLA_CHUNK, GLA_CHUNK), 0)
    c = lax.broadcasted_iota(jnp.int32, (GLA_CHUNK, GLA_CHUNK), 1)
    causal = r >= c
    ones_tril = jnp.where(causal, 1.0, 0.0).astype(BF16)
    nt = (((1,), (1,)), ((), ()))
    tn = (((0,), (0,)), ((), ()))

    def chunk(ci, carry):
        rs = pl.ds(pl.multiple_of(ci * GLA_CHUNK, GLA_CHUNK), GLA_CHUNK)
        z = z_ref[rs, :] + ba_ref[...]
        la = (jnp.minimum(z, 0.0) - jnp.log1p(jnp.exp(-jnp.abs(z)))) * (1.0 / GLA_TAU)
        la_hi = la.astype(BF16)
        la_lo = (la - la_hi.astype(F32)).astype(BF16)
        b = (jnp.dot(ones_tril, la_hi, preferred_element_type=F32)
             + jnp.dot(ones_tril, la_lo, preferred_element_type=F32))
        b_last = b[GLA_CHUNK - 1:GLA_CHUNK, :]
        q = q_ref[rs, :].astype(F32) * (GLA_DK ** -0.5)
        k = k_ref[rs, :].astype(F32)
        q_dec = (q * jnp.exp(b)).astype(BF16)
        k_inv = (k * jnp.exp(-b)).astype(BF16)
        k_dec = (k * jnp.exp(b_last - b)).astype(BF16)
        decay = jnp.exp(b_last)
        for h in range(GLA_HEADS):
            ks = slice(h * GLA_DK, (h + 1) * GLA_DK)
            vs = slice(h * GLA_DV, (h + 1) * GLA_DV)
            scores = lax.dot_general(q_dec[:, ks], k_inv[:, ks], nt, preferred_element_type=F32)
            scores = jnp.where(causal, scores, 0.0).astype(BF16)
            vh = v_ref[rs, vs]
            state_t = st_ref[h]
            o = (jnp.dot(scores, vh, preferred_element_type=F32)
                 + lax.dot_general(q_dec[:, ks], state_t.astype(BF16), nt, preferred_element_type=F32))
            st_ref[h] = (state_t * decay[:, ks]
                         + lax.dot_general(vh, k_dec[:, ks], tn, preferred_element_type=F32))
            o = o * lax.rsqrt(jnp.mean(o * o, axis=-1, keepdims=True) + RMS_EPS) * ng_ref[:, vs]
            gate = g_ref[rs, vs].astype(F32)
            o_ref[rs, vs] = (o * (gate * jax.nn.sigmoid(gate))).astype(BF16)
        return carry

    lax.fori_loop(0, rows // GLA_CHUNK, chunk, 0)


def _gla(proj, z, ba, norm_g, batch, seq):
    t = proj.shape[0]
    nseq = seq // ROW_TILE
    kw, vw = GLA_KEY_WIDTH, GLA_WIDTH
    q_blk = (2 * GM_WIDTH) // kw
    v_blk = (2 * GM_WIDTH + 2 * kw) // vw
    row = lambda b, s: b * nseq + s
    return pl.pallas_call(
        _gla_kernel,
        grid=(batch, nseq),
        in_specs=[pl.BlockSpec((ROW_TILE, kw), lambda b, s: (row(b, s), q_blk)),
                  pl.BlockSpec((ROW_TILE, kw), lambda b, s: (row(b, s), q_blk + 1)),
                  pl.BlockSpec((ROW_TILE, vw), lambda b, s: (row(b, s), v_blk)),
                  pl.BlockSpec((ROW_TILE, vw), lambda b, s: (row(b, s), v_blk + 1)),
                  pl.BlockSpec((ROW_TILE, kw), lambda b, s: (row(b, s), 0)),
                  pl.BlockSpec((1, kw), lambda b, s: (0, 0)),
                  pl.BlockSpec((1, vw), lambda b, s: (0, 0))],
        out_specs=pl.BlockSpec((ROW_TILE, vw), lambda b, s: (row(b, s), 0)),
        out_shape=jax.ShapeDtypeStruct((t, vw), BF16),
        scratch_shapes=[pltpu.VMEM((GLA_HEADS, GLA_DV, GLA_DK), F32)],
        compiler_params=_params("parallel", "arbitrary"),
        name="gla",
    )(proj, proj, proj, proj, z, ba, norm_g)


def _outproj_ln_kernel(a_ref, b_ref, w_ref, x_ref, g_ref, be_ref, o_ref):
    h = (jnp.dot(a_ref[...], w_ref[:GM_WIDTH, :], preferred_element_type=F32)
         + jnp.dot(b_ref[...], w_ref[GM_WIDTH:, :], preferred_element_type=F32))
    o_ref[...] = _layer_norm(DEEPNORM_ALPHA * x_ref[...] + h, g_ref[...], be_ref[...])


def _outproj_ln(out_a, out_b, w, x, g, b):
    t, d = x.shape
    return pl.pallas_call(
        _outproj_ln_kernel,
        grid=(t // ROW_TILE,),
        in_specs=[pl.BlockSpec((ROW_TILE, GM_WIDTH), lambda i: (i, 0)),
                  pl.BlockSpec((ROW_TILE, GLA_WIDTH), lambda i: (i, 0)),
                  pl.BlockSpec((GM_WIDTH + GLA_WIDTH, d), lambda i: (0, 0)),
                  pl.BlockSpec((ROW_TILE, d), lambda i: (i, 0)),
                  pl.BlockSpec((1, d), lambda i: (0, 0)),
                  pl.BlockSpec((1, d), lambda i: (0, 0))],
        out_specs=pl.BlockSpec((ROW_TILE, d), lambda i: (i, 0)),
        out_shape=jax.ShapeDtypeStruct((t, d), F32),
        compiler_params=_params("parallel"),
        name="outproj_ln",
    )(out_a, out_b, w, x, g, b)


def _swiglu_kernel(te_ref, nu_ref, x_ref, wg_ref, wu_ref, wd_ref, *rest, fuse_ln):
    if fuse_ln:
        g_ref, b_ref, o_ref, xb_ref, acc_ref = rest
    else:
        o_ref, xb_ref, acc_ref = rest
    i = pl.program_id(0)
    j = pl.program_id(1)
    nj = pl.num_programs(1)

    @pl.when(i < nu_ref[0])
    def _():
        @pl.when(j == 0)
        def _():
            xb_ref[...] = x_ref[...].astype(BF16)
            acc_ref[...] = jnp.zeros(acc_ref.shape, F32)

        xb = xb_ref[...]
        gate = jnp.dot(xb, wg_ref[...], preferred_element_type=F32)
        up = jnp.dot(xb, wu_ref[...], preferred_element_type=F32)
        hid = (gate * jax.nn.sigmoid(gate) * up).astype(BF16)
        acc_ref[...] += jnp.dot(hid, wd_ref[...], preferred_element_type=F32)

        @pl.when(j == nj - 1)
        def _():
            if fuse_ln:
                o_ref[...] = _layer_norm(DEEPNORM_ALPHA * x_ref[...] + acc_ref[...],
                                         g_ref[...], b_ref[...])
            else:
                o_ref[...] = acc_ref[...]

    if not fuse_ln:
        @pl.when(jnp.logical_and(i >= nu_ref[0], j == nj - 1))
        def _():
            o_ref[...] = jnp.zeros(o_ref.shape, o_ref.dtype)


def _swiglu(tile_expert, n_used, x, wg, wu, wd, ln=None, *, ff_tile):
    rows, d = x.shape
    ff = wg.shape[2]
    n_tiles = rows // FFN_ROW_TILE
    nj = ff // ff_tile

    def jj(i, j, nu):
        return jnp.where(i < nu[0], j, nj - 1)

    in_specs = [pl.BlockSpec((FFN_ROW_TILE, d), lambda i, j, te, nu: (i, 0)),
                pl.BlockSpec((None, d, ff_tile), lambda i, j, te, nu: (te[i], 0, jj(i, j, nu))),
                pl.BlockSpec((None, d, ff_tile), lambda i, j, te, nu: (te[i], 0, jj(i, j, nu))),
                pl.BlockSpec((None, ff_tile, d), lambda i, j, te, nu: (te[i], jj(i, j, nu), 0))]
    args = [x, wg, wu, wd]
    if ln is not None:
        in_specs += [pl.BlockSpec((1, d), lambda i, j, te, nu: (0, 0)),
                     pl.BlockSpec((1, d), lambda i, j, te, nu: (0, 0))]
        args += list(ln)
    return pl.pallas_call(
        functools.partial(_swiglu_kernel, fuse_ln=ln is not None),
        grid_spec=pltpu.PrefetchScalarGridSpec(
            num_scalar_prefetch=2,
            grid=(n_tiles, nj),
            in_specs=in_specs,
            out_specs=pl.BlockSpec((FFN_ROW_TILE, d), lambda i, j, te, nu: (i, 0)),
            scratch_shapes=[pltpu.VMEM((FFN_ROW_TILE, d), BF16),
                            pltpu.VMEM((FFN_ROW_TILE, d), F32)]),
        out_shape=jax.ShapeDtypeStruct((rows, d), F32),
        compiler_params=_params("parallel", "arbitrary"),
        name="swiglu_ln" if ln is not None else "swiglu_experts",
    )(tile_expert, n_used, *args)


def _router_kernel(x_ref, wr_ref, ri_ref, rw_ref, cnt_ref, carry_ref):
    tb = x_ref.shape[0]

    @pl.when(pl.program_id(0) == 0)
    def _():
        carry_ref[...] = jnp.zeros(carry_ref.shape, F32)

    logits = lax.dot_general(wr_ref[...], x_ref[...], (((1,), (1,)), ((), ())),
                             precision=lax.Precision.HIGHEST, preferred_element_type=F32)
    e_iota = lax.broadcasted_iota(jnp.int32, (N_EXPERTS, tb), 0)
    m0 = jnp.max(logits, axis=0, keepdims=True)
    i0 = jnp.min(jnp.where(logits == m0, e_iota, N_EXPERTS), axis=0, keepdims=True)
    rest = jnp.where(e_iota == i0, -jnp.inf, logits)
    m1 = jnp.max(rest, axis=0, keepdims=True)
    i1 = jnp.min(jnp.where(rest == m1, e_iota, N_EXPERTS), axis=0, keepdims=True)
    ex = jnp.exp(m1 - m0)
    w0 = 1.0 / (1.0 + ex)
    w1 = ex / (1.0 + ex)
    oh0 = e_iota == i0
    oh1 = e_iota == i1
    onehot = jnp.where(jnp.logical_or(oh0, oh1), 1.0, 0.0)
    tr = lax.broadcasted_iota(jnp.int32, (tb, tb), 0)
    tc = lax.broadcasted_iota(jnp.int32, (tb, tb), 1)
    before = jnp.where(tr < tc, 1.0, 0.0).astype(BF16)
    cum = jnp.dot(onehot.astype(BF16), before, preferred_element_type=F32) + carry_ref[:, 0:1]
    rank0 = jnp.sum(jnp.where(oh0, cum, 0.0), axis=0, keepdims=True)
    rank1 = jnp.sum(jnp.where(oh1, cum, 0.0), axis=0, keepdims=True)
    carry_ref[...] = carry_ref[...] + jnp.sum(onehot, axis=1, keepdims=True)
    cnt_ref[...] = carry_ref[...]
    ri_ref[...] = jnp.zeros(ri_ref.shape, jnp.int32)
    ri_ref[0:1, :] = i0
    ri_ref[1:2, :] = i1
    ri_ref[2:3, :] = rank0.astype(jnp.int32)
    ri_ref[3:4, :] = rank1.astype(jnp.int32)
    rw_ref[...] = jnp.zeros(rw_ref.shape, F32)
    rw_ref[0:1, :] = w0
    rw_ref[1:2, :] = w1


def _router(x, wr_t):
    t, d = x.shape
    return pl.pallas_call(
        _router_kernel,
        grid=(t // ROUTER_TILE,),
        in_specs=[pl.BlockSpec((ROUTER_TILE, d), lambda i: (i, 0)),
                  pl.BlockSpec((N_EXPERTS, d), lambda i: (0, 0))],
        out_specs=[pl.BlockSpec((SUBLANE, ROUTER_TILE), lambda i: (0, i)),
                   pl.BlockSpec((SUBLANE, ROUTER_TILE), lambda i: (0, i)),
                   pl.BlockSpec((N_EXPERTS, LANE), lambda i: (0, 0))],
        out_shape=[jax.ShapeDtypeStruct((SUBLANE, t), jnp.int32),
                   jax.ShapeDtypeStruct((SUBLANE, t), F32),
                   jax.ShapeDtypeStruct((N_EXPERTS, LANE), F32)],
        scratch_shapes=[pltpu.VMEM((N_EXPERTS, LANE), F32)],
        compiler_params=_params("arbitrary"),
        name="router",
    )(x, wr_t)


def _row_copy(src, dst, sem, s, d):
    return pltpu.make_async_copy(src.at[s], dst.at[d], sem)


def _scatter_kernel(pos_ref, x_hbm, init_hbm, o_hbm, sem, *, n_tok):
    del init_hbm
    base = pl.program_id(0) * MOVE_TILE

    def issue(t, carry):
        tok = base + t
        _row_copy(x_hbm, o_hbm, sem, tok, pos_ref[tok]).start()
        _row_copy(x_hbm, o_hbm, sem, tok, pos_ref[n_tok + tok]).start()
        return carry

    def drain(t, carry):
        _row_copy(x_hbm, o_hbm, sem, 0, 0).wait()
        _row_copy(x_hbm, o_hbm, sem, 0, 0).wait()
        return carry

    lax.fori_loop(0, MOVE_TILE, issue, 0)
    lax.fori_loop(0, MOVE_TILE, drain, 0)


def _scatter_rows(pos, x3, n_rows):
    n_tok = x3.shape[0]
    init = jnp.zeros((n_rows,) + x3.shape[1:], x3.dtype)
    return pl.pallas_call(
        functools.partial(_scatter_kernel, n_tok=n_tok),
        grid_spec=pltpu.PrefetchScalarGridSpec(
            num_scalar_prefetch=1,
            grid=(n_tok // MOVE_TILE,),
            in_specs=[pl.BlockSpec(memory_space=pl.ANY), pl.BlockSpec(memory_space=pl.ANY)],
            out_specs=pl.BlockSpec(memory_space=pl.ANY),
            scratch_shapes=[pltpu.SemaphoreType.DMA(())]),
        out_shape=jax.ShapeDtypeStruct(init.shape, init.dtype),
        input_output_aliases={2: 0},
        compiler_params=pltpu.CompilerParams(dimension_semantics=("arbitrary",),
                                             has_side_effects=True),
        name="scatter_rows",
    )(pos, x3, init)


def _gather_kernel(pos_ref, y_hbm, o_hbm, sem, *, n_tok):
    base = pl.program_id(0) * MOVE_TILE

    def issue(t, carry):
        tok = base + t
        _row_copy(y_hbm, o_hbm, sem, pos_ref[tok], tok).start()
        _row_copy(y_hbm, o_hbm, sem, pos_ref[n_tok + tok], n_tok + tok).start()
        return carry

    def drain(t, carry):
        _row_copy(y_hbm, o_hbm, sem, 0, 0).wait()
        _row_copy(y_hbm, o_hbm, sem, 0, 0).wait()
        return carry

    lax.fori_loop(0, MOVE_TILE, issue, 0)
    lax.fori_loop(0, MOVE_TILE, drain, 0)


def _gather_rows(pos, y3, n_tok):
    return pl.pallas_call(
        functools.partial(_gather_kernel, n_tok=n_tok),
        grid_spec=pltpu.PrefetchScalarGridSpec(
            num_scalar_prefetch=1,
            grid=(n_tok // MOVE_TILE,),
            in_specs=[pl.BlockSpec(memory_space=pl.ANY)],
            out_specs=pl.BlockSpec(memory_space=pl.ANY),
            scratch_shapes=[pltpu.SemaphoreType.DMA(())]),
        out_shape=jax.ShapeDtypeStruct((TOP_K * n_tok,) + y3.shape[1:], y3.dtype),
        compiler_params=pltpu.CompilerParams(dimension_semantics=("arbitrary",),
                                             has_side_effects=True),
        name="gather_rows",
    )(pos, y3)


def _combine_ln_kernel(y0_ref, y1_ref, w_ref, x_ref, g_ref, b_ref, o_ref):
    f = w_ref[:, 0:1] * y0_ref[...] + w_ref[:, 1:2] * y1_ref[...]
    o_ref[...] = _layer_norm(DEEPNORM_ALPHA * x_ref[...] + f, g_ref[...], b_ref[...])


def _combine_ln(yg, w_col, x, g, b):
    t, d = x.shape
    nt = t // ROW_TILE
    return pl.pallas_call(
        _combine_ln_kernel,
        grid=(nt,),
        in_specs=[pl.BlockSpec((ROW_TILE, d), lambda i: (i, 0)),
                  pl.BlockSpec((ROW_TILE, d), lambda i: (nt + i, 0)),
                  pl.BlockSpec((ROW_TILE, TOP_K), lambda i: (i, 0)),
                  pl.BlockSpec((ROW_TILE, d), lambda i: (i, 0)),
                  pl.BlockSpec((1, d), lambda i: (0, 0)),
                  pl.BlockSpec((1, d), lambda i: (0, 0))],
        out_specs=pl.BlockSpec((ROW_TILE, d), lambda i: (i, 0)),
        out_shape=jax.ShapeDtypeStruct((t, d), F32),
        compiler_params=_params("parallel"),
        name="combine_ln",
    )(yg, yg, w_col, x, g, b)


def _mixer(x, batch, seq, w_in, gm_ws, gm_bs, gm_ln_g, gm_ln_b, gla_wa2, gla_ba, gla_norm_g,
           w_out, ln_g, ln_b):
    w_gate = _fold_gate(w_in[:, PROJ_MAIN:], gla_wa2)
    w_ext = jnp.concatenate([w_in[:, :PROJ_MAIN], w_gate], axis=1).astype(BF16)
    proj, z = _inproj(x, w_ext)
    out_a = _gmlp(proj, gm_ws, gm_bs.reshape(GM_HEADS, GM_CHUNK, 1),
                  gm_ln_g.reshape(1, -1), gm_ln_b.reshape(1, -1))
    out_b = _gla(proj, z, gla_ba.reshape(1, -1), gla_norm_g.reshape(1, -1), batch, seq)
    return _outproj_ln(out_a, out_b, w_out.astype(BF16), x, ln_g.reshape(1, -1), ln_b.reshape(1, -1))


def _dense_ffn(x, wg, wu, wd, ln_g, ln_b):
    n_tiles = x.shape[0] // FFN_ROW_TILE
    te = jnp.zeros((n_tiles,), jnp.int32)
    nu = jnp.full((1,), n_tiles, jnp.int32)
    return _swiglu(te, nu, x, wg.astype(BF16)[None], wu.astype(BF16)[None], wd.astype(BF16)[None],
                   (ln_g.reshape(1, -1), ln_b.reshape(1, -1)), ff_tile=DENSE_FF_TILE)


def _moe_ffn(x, w_router, e_gate, e_up, e_down, ln_g, ln_b):
    t, d = x.shape
    tm = FFN_ROW_TILE
    n_tiles = (TOP_K * t + N_EXPERTS * (tm - 1)) // tm
    n_rows = n_tiles * tm
    ri, rw, cnt = _router(x, w_router.T)
    counts = cnt[:, 0].astype(jnp.int32)
    group = ((counts + tm - 1) // tm) * tm
    ends = jnp.cumsum(group)
    offs = ends - group
    pos = jnp.concatenate([offs[ri[0]] + ri[2], offs[ri[1]] + ri[3]])
    tile_start = jnp.arange(n_tiles, dtype=jnp.int32) * tm
    tile_expert = jnp.minimum(jnp.sum(tile_start[:, None] >= ends[None, :], axis=1),
                              N_EXPERTS - 1).astype(jnp.int32)
    n_used = (ends[-1:] // tm).astype(jnp.int32)
    x3 = x.reshape(t, SUBLANE, LANE)
    xs = _scatter_rows(pos, x3, n_rows).reshape(n_rows, d)
    ys = _swiglu(tile_expert, n_used, xs, e_gate.astype(BF16), e_up.astype(BF16),
                 e_down.astype(BF16), ff_tile=MOE_FF_TILE)
    yg = _gather_rows(pos, ys.reshape(n_rows, SUBLANE, LANE), t).reshape(TOP_K * t, d)
    return _combine_ln(yg, rw[:TOP_K].T, x, ln_g.reshape(1, -1), ln_b.reshape(1, -1))


def kernel(x, w_in, gm_ws, gm_bs, gm_ln_g, gm_ln_b, gla_wa2, gla_ba, gla_norm_g, w_out, ln_mix_g, ln_mix_b, ffn_w_gate, ffn_w_up, ffn_w_down, router_w, exp_w_gate, exp_w_up, exp_w_down, ln_ffn_g, ln_ffn_b):
    batch, seq, d = x.shape
    h = x.reshape(batch * seq, d)
    for layer in range(DEPTH):
        h = _mixer(h, batch, seq, w_in[layer], gm_ws[layer], gm_bs[layer], gm_ln_g[layer],
                   gm_ln_b[layer], gla_wa2[layer], gla_ba[layer], gla_norm_g[layer], w_out[layer],
                   ln_mix_g[layer], ln_mix_b[layer])
        i = layer // 2
        if layer % 2 == 0:
            h = _dense_ffn(h, ffn_w_gate[i], ffn_w_up[i], ffn_w_down[i],
                           ln_ffn_g[layer], ln_ffn_b[layer])
        else:
            h = _moe_ffn(h, router_w[i], exp_w_gate[i], exp_w_up[i], exp_w_down[i],
                         ln_ffn_g[layer], ln_ffn_b[layer])
    return h.reshape(batch, seq, d)
```

```python
import functools

import jax
import jax.numpy as jnp
from jax import lax
from jax.experimental import pallas as pl
from jax.experimental.pallas import tpu as pltpu

F32 = jnp.float32
BF16 = jnp.bfloat16

D_MODEL = 1024
DEPTH = 2
GM_HEADS = 4
GM_WIDTH = 512
GM_HEAD_DIM = 128
GM_CHUNK = 128
GLA_HEADS = 4
GLA_WIDTH = 512
GLA_DV = 128
GLA_KEY_WIDTH = 256
GLA_DK = 64
GLA_GATE_RANK = 16
GLA_TAU = 16.0
GLA_CHUNK = 64
PROJ_MAIN = 2 * GM_WIDTH + 2 * GLA_KEY_WIDTH + 2 * GLA_WIDTH
N_EXPERTS = 8
TOP_K = 2
LN_EPS = 1e-5
RMS_EPS = 1e-6
DEEPNORM_ALPHA = (2 * DEPTH) ** 0.25

V7X_VMEM_LIMIT_BYTES = 56 * 1024 * 1024
LANE = 128
SUBLANE = 8
MXU_N = 256

ROW_TILE = 512
FFN_ROW_TILE = 1024
DENSE_FF_TILE = 256
MOE_FF_TILE = 512
MOE_BLOCK = 512
GRANULE = 16
LOC_ROWS = -(-(TOP_K * MOE_BLOCK + N_EXPERTS * (GRANULE - 1)) // LANE) * LANE
GRAN_PER_TILE = FFN_ROW_TILE // GRANULE


def _params(*sem):
    return pltpu.CompilerParams(dimension_semantics=sem, vmem_limit_bytes=V7X_VMEM_LIMIT_BYTES)


def _gelu(x):
    return 0.5 * x * (1.0 + lax.erf(x * (2.0 ** -0.5)))


def _layer_norm(y, g, b):
    mu = jnp.mean(y, axis=-1, keepdims=True)
    d = y - mu
    var = jnp.mean(d * d, axis=-1, keepdims=True)
    return d * lax.rsqrt(var + LN_EPS) * g + b


def _fold_kernel(w1_ref, w2_ref, o_ref):
    acc = jnp.zeros(o_ref.shape, F32)
    for r in range(GLA_GATE_RANK):
        acc = acc + w1_ref[:, r:r + 1] * w2_ref[r:r + 1, :]
    o_ref[...] = acc


def _fold_gate(w1, w2):
    return pl.pallas_call(
        _fold_kernel,
        out_shape=jax.ShapeDtypeStruct((w1.shape[0], w2.shape[1]), F32),
        name="fold_gate",
    )(w1, w2)


def _inproj_kernel(x_ref, w_ref, p_ref, z_ref):
    xb = x_ref[...].astype(BF16)
    for n in range(PROJ_MAIN // MXU_N):
        cols = slice(n * MXU_N, (n + 1) * MXU_N)
        p_ref[:, cols] = jnp.dot(xb, w_ref[:, cols], preferred_element_type=F32).astype(BF16)
    z_ref[...] = jnp.dot(xb, w_ref[:, PROJ_MAIN:], preferred_element_type=F32)


def _inproj(x, w):
    t, d = x.shape
    n = w.shape[1]
    return pl.pallas_call(
        _inproj_kernel,
        grid=(t // ROW_TILE,),
        in_specs=[pl.BlockSpec((ROW_TILE, d), lambda i: (i, 0)),
                  pl.BlockSpec((d, n), lambda i: (0, 0))],
        out_specs=[pl.BlockSpec((ROW_TILE, PROJ_MAIN), lambda i: (i, 0)),
                   pl.BlockSpec((ROW_TILE, GLA_KEY_WIDTH), lambda i: (i, 0))],
        out_shape=[jax.ShapeDtypeStruct((t, PROJ_MAIN), BF16),
                   jax.ShapeDtypeStruct((t, GLA_KEY_WIDTH), F32)],
        compiler_params=_params("parallel"),
        name="inproj",
    )(x, w)


def _gmlp_kernel(uv_ref, w_ref, bcol_ref, g_ref, b_ref, o_ref):
    rows = uv_ref.shape[0]
    r = lax.broadcasted_iota(jnp.int32, (GM_CHUNK, GM_CHUNK), 0)
    c = lax.broadcasted_iota(jnp.int32, (GM_CHUNK, GM_CHUNK), 1)
    causal = r >= c
    for h in range(GM_HEADS):
        cols = slice(h * GM_HEAD_DIM, (h + 1) * GM_HEAD_DIM)
        vcols = slice(GM_WIDTH + h * GM_HEAD_DIM, GM_WIDTH + (h + 1) * GM_HEAD_DIM)
        wm = jnp.where(causal, w_ref[h], 0.0).astype(BF16)
        v = _gelu(uv_ref[:, vcols].astype(F32))
        vn = _layer_norm(v, g_ref[:, cols], b_ref[:, cols]).astype(BF16)
        u = _gelu(uv_ref[:, cols].astype(F32))
        for ci in range(rows // GM_CHUNK):
            rs = slice(ci * GM_CHUNK, (ci + 1) * GM_CHUNK)
            s = jnp.dot(wm, vn[rs], preferred_element_type=F32) + bcol_ref[h]
            o_ref[rs, cols] = (u[rs] * s).astype(BF16)


def _gmlp(proj, w_s, b_col, ln_g, ln_b):
    t = proj.shape[0]
    return pl.pallas_call(
        _gmlp_kernel,
        grid=(t // ROW_TILE,),
        in_specs=[pl.BlockSpec((ROW_TILE, 2 * GM_WIDTH), lambda i: (i, 0)),
                  pl.BlockSpec((GM_HEADS, GM_CHUNK, GM_CHUNK), lambda i: (0, 0, 0)),
                  pl.BlockSpec((GM_HEADS, GM_CHUNK, 1), lambda i: (0, 0, 0)),
                  pl.BlockSpec((1, GM_WIDTH), lambda i: (0, 0)),
                  pl.BlockSpec((1, GM_WIDTH), lambda i: (0, 0))],
        out_specs=pl.BlockSpec((ROW_TILE, GM_WIDTH), lambda i: (i, 0)),
        out_shape=jax.ShapeDtypeStruct((t, GM_WIDTH), BF16),
        compiler_params=_params("parallel"),
        name="gmlp",
    )(proj, w_s, b_col, ln_g, ln_b)


def _gla_kernel(q_ref, k_ref, v_ref, g_ref, z_ref, ba_ref, ng_ref, o_ref, st_ref):
    rows = q_ref.shape[0]

    @pl.when(pl.program_id(1) == 0)
    def _():
        st_ref[...] = jnp.zeros(st_ref.shape, F32)

    r = lax.broadcasted_iota(jnp.int32, (GLA_CHUNK, GLA_CHUNK), 0)
    c = lax.broadcasted_iota(jnp.int32, (GLA_CHUNK, GLA_CHUNK), 1)
    causal = r >= c
    ones_tril = jnp.where(causal, 1.0, 0.0).astype(BF16)
    nt = (((1,), (1,)), ((), ()))
    tn = (((0,), (0,)), ((), ()))

    def chunk(ci, carry):
        rs = pl.ds(pl.multiple_of(ci * GLA_CHUNK, GLA_CHUNK), GLA_CHUNK)
        z = z_ref[rs, :] + ba_ref[...]
        la = (jnp.minimum(z, 0.0) - jnp.log1p(jnp.exp(-jnp.abs(z)))) * (1.0 / GLA_TAU)
        la_hi = la.astype(BF16)
        la_lo = (la - la_hi.astype(F32)).astype(BF16)
        b = (jnp.dot(ones_tril, la_hi, preferred_element_type=F32)
             + jnp.dot(ones_tril, la_lo, preferred_element_type=F32))
        b_last = b[GLA_CHUNK - 1:GLA_CHUNK, :]
        q = q_ref[rs, :].astype(F32) * (GLA_DK ** -0.5)
        k = k_ref[rs, :].astype(F32)
        q_dec = (q * jnp.exp(b)).astype(BF16)
        k_inv = (k * jnp.exp(-b)).astype(BF16)
        k_dec = (k * jnp.exp(b_last - b)).astype(BF16)
        decay = jnp.exp(b_last)
        for h in range(GLA_HEADS):
            ks = slice(h * GLA_DK, (h + 1) * GLA_DK)
            vs = slice(h * GLA_DV, (h + 1) * GLA_DV)
            scores = lax.dot_general(q_dec[:, ks], k_inv[:, ks], nt, preferred_element_type=F32)
            scores = jnp.where(causal, scores, 0.0).astype(BF16)
            vh = v_ref[rs, vs]
            state_t = st_ref[h]
            o = (jnp.dot(scores, vh, preferred_element_type=F32)
                 + lax.dot_general(q_dec[:, ks], state_t.astype(BF16), nt, preferred_element_type=F32))
            st_ref[h] = (state_t * decay[:, ks]
                         + lax.dot_general(vh, k_dec[:, ks], tn, preferred_element_type=F32))
            o = o * lax.rsqrt(jnp.mean(o * o, axis=-1, keepdims=True) + RMS_EPS) * ng_ref[:, vs]
            gate = g_ref[rs, vs].astype(F32)
            o_ref[rs, vs] = (o * (gate * jax.nn.sigmoid(gate))).astype(BF16)
        return carry

    lax.fori_loop(0, rows // GLA_CHUNK, chunk, 0)


def _gla(proj, z, ba, norm_g, batch, seq):
    t = proj.shape[0]
    nseq = seq // ROW_TILE
    kw, vw = GLA_KEY_WIDTH, GLA_WIDTH
    q_blk = (2 * GM_WIDTH) // kw
    v_blk = (2 * GM_WIDTH + 2 * kw) // vw
    row = lambda b, s: b * nseq + s
    return pl.pallas_call(
        _gla_kernel,
        grid=(batch, nseq),
        in_specs=[pl.BlockSpec((ROW_TILE, kw), lambda b, s: (row(b, s), q_blk)),
                  pl.BlockSpec((ROW_TILE, kw), lambda b, s: (row(b, s), q_blk + 1)),
                  pl.BlockSpec((ROW_TILE, vw), lambda b, s: (row(b, s), v_blk)),
                  pl.BlockSpec((ROW_TILE, vw), lambda b, s: (row(b, s), v_blk + 1)),
                  pl.BlockSpec((ROW_TILE, kw), lambda b, s: (row(b, s), 0)),
                  pl.BlockSpec((1, kw), lambda b, s: (0, 0)),
                  pl.BlockSpec((1, vw), lambda b, s: (0, 0))],
        out_specs=pl.BlockSpec((ROW_TILE, vw), lambda b, s: (row(b, s), 0)),
        out_shape=jax.ShapeDtypeStruct((t, vw), BF16),
        scratch_shapes=[pltpu.VMEM((GLA_HEADS, GLA_DV, GLA_DK), F32)],
        compiler_params=_params("parallel", "arbitrary"),
        name="gla",
    )(proj, proj, proj, proj, z, ba, norm_g)


def _outproj_ln_kernel(a_ref, b_ref, w_ref, x_ref, g_ref, be_ref, o_ref):
    h = (jnp.dot(a_ref[...], w_ref[:GM_WIDTH, :], preferred_element_type=F32)
         + jnp.dot(b_ref[...], w_ref[GM_WIDTH:, :], preferred_element_type=F32))
    o_ref[...] = _layer_norm(DEEPNORM_ALPHA * x_ref[...] + h, g_ref[...], be_ref[...])


def _outproj_ln(out_a, out_b, w, x, g, b):
    t, d = x.shape
    return pl.pallas_call(
        _outproj_ln_kernel,
        grid=(t // ROW_TILE,),
        in_specs=[pl.BlockSpec((ROW_TILE, GM_WIDTH), lambda i: (i, 0)),
                  pl.BlockSpec((ROW_TILE, GLA_WIDTH), lambda i: (i, 0)),
                  pl.BlockSpec((GM_WIDTH + GLA_WIDTH, d), lambda i: (0, 0)),
                  pl.BlockSpec((ROW_TILE, d), lambda i: (i, 0)),
                  pl.BlockSpec((1, d), lambda i: (0, 0)),
                  pl.BlockSpec((1, d), lambda i: (0, 0))],
        out_specs=pl.BlockSpec((ROW_TILE, d), lambda i: (i, 0)),
        out_shape=jax.ShapeDtypeStruct((t, d), F32),
        compiler_params=_params("parallel"),
        name="outproj_ln",
    )(out_a, out_b, w, x, g, b)


def _swiglu_step(xb, wg_ref, wu_ref, wd_ref):
    gate = jnp.dot(xb, wg_ref[...], preferred_element_type=F32)
    up = jnp.dot(xb, wu_ref[...], preferred_element_type=F32)
    hid = (gate * jax.nn.sigmoid(gate) * up).astype(BF16)
    return jnp.dot(hid, wd_ref[...], preferred_element_type=F32)


def _swiglu_ln_kernel(x_ref, wg_ref, wu_ref, wd_ref, g_ref, b_ref, o_ref, xb_ref, acc_ref):
    j = pl.program_id(1)

    @pl.when(j == 0)
    def _():
        xb_ref[...] = x_ref[...].astype(BF16)
        acc_ref[...] = jnp.zeros(acc_ref.shape, F32)

    acc_ref[...] += _swiglu_step(xb_ref[...], wg_ref, wu_ref, wd_ref)

    @pl.when(j == pl.num_programs(1) - 1)
    def _():
        o_ref[...] = _layer_norm(DEEPNORM_ALPHA * x_ref[...] + acc_ref[...], g_ref[...], b_ref[...])


def _swiglu_ln(x, wg, wu, wd, g, b):
    rows, d = x.shape
    ff = wg.shape[1]
    return pl.pallas_call(
        _swiglu_ln_kernel,
        grid=(rows // FFN_ROW_TILE, ff // DENSE_FF_TILE),
        in_specs=[pl.BlockSpec((FFN_ROW_TILE, d), lambda i, j: (i, 0)),
                  pl.BlockSpec((d, DENSE_FF_TILE), lambda i, j: (0, j)),
                  pl.BlockSpec((d, DENSE_FF_TILE), lambda i, j: (0, j)),
                  pl.BlockSpec((DENSE_FF_TILE, d), lambda i, j: (j, 0)),
                  pl.BlockSpec((1, d), lambda i, j: (0, 0)),
                  pl.BlockSpec((1, d), lambda i, j: (0, 0))],
        out_specs=pl.BlockSpec((FFN_ROW_TILE, d), lambda i, j: (i, 0)),
        out_shape=jax.ShapeDtypeStruct((rows, d), F32),
        scratch_shapes=[pltpu.VMEM((FFN_ROW_TILE, d), BF16), pltpu.VMEM((FFN_ROW_TILE, d), F32)],
        compiler_params=_params("parallel", "arbitrary"),
        name="swiglu_ln",
    )(x, wg, wu, wd, g, b)


def _router_kernel(x_ref, wr_ref, ri_ref, rw_ref, cnt_ref):
    tb = x_ref.shape[0]
    logits = lax.dot_general(wr_ref[...], x_ref[...], (((1,), (1,)), ((), ())),
                             precision=lax.Precision.HIGHEST, preferred_element_type=F32)
    e_iota = lax.broadcasted_iota(jnp.int32, (N_EXPERTS, tb), 0)
    m0 = jnp.max(logits, axis=0, keepdims=True)
    i0 = jnp.min(jnp.where(logits == m0, e_iota, N_EXPERTS), axis=0, keepdims=True)
    rest = jnp.where(e_iota == i0, -jnp.inf, logits)
    m1 = jnp.max(rest, axis=0, keepdims=True)
    i1 = jnp.min(jnp.where(rest == m1, e_iota, N_EXPERTS), axis=0, keepdims=True)
    ex = jnp.exp(m1 - m0)
    w0 = 1.0 / (1.0 + ex)
    w1 = ex / (1.0 + ex)
    oh0 = e_iota == i0
    oh1 = e_iota == i1
    onehot = jnp.where(jnp.logical_or(oh0, oh1), 1.0, 0.0)
    tr = lax.broadcasted_iota(jnp.int32, (tb, tb), 0)
    tc = lax.broadcasted_iota(jnp.int32, (tb, tb), 1)
    before = jnp.where(tr < tc, 1.0, 0.0).astype(BF16)
    rank = jnp.dot(onehot.astype(BF16), before, preferred_element_type=F32)
    count = jnp.broadcast_to(jnp.sum(onehot, axis=1, keepdims=True), (N_EXPERTS, LANE))
    cap = jnp.ceil(count * (1.0 / GRANULE)) * GRANULE
    e_sub = lax.broadcasted_iota(jnp.int32, (N_EXPERTS, LANE), 0)
    seg = jnp.zeros((N_EXPERTS, LANE), F32)
    for e in range(N_EXPERTS - 1):
        seg = seg + jnp.where(e_sub > e, cap[e:e + 1, :], 0.0)
    row = rank + seg[:, 0:1]
    row0 = jnp.sum(jnp.where(oh0, row, 0.0), axis=0, keepdims=True)
    row1 = jnp.sum(jnp.where(oh1, row, 0.0), axis=0, keepdims=True)
    cnt_ref[...] = count
    ri_ref[...] = jnp.zeros(ri_ref.shape, jnp.int32)
    ri_ref[0:1, :] = i0
    ri_ref[1:2, :] = i1
    ri_ref[2:3, :] = row0.astype(jnp.int32)
    ri_ref[3:4, :] = row1.astype(jnp.int32)
    rw_ref[...] = jnp.zeros(rw_ref.shape, F32)
    rw_ref[0:1, :] = w0
    rw_ref[1:2, :] = w1


def _router(x, wr_t):
    t, d = x.shape
    nb = t // MOE_BLOCK
    return pl.pallas_call(
        _router_kernel,
        grid=(nb,),
        in_specs=[pl.BlockSpec((MOE_BLOCK, d), lambda i: (i, 0)),
                  pl.BlockSpec((N_EXPERTS, d), lambda i: (0, 0))],
        out_specs=[pl.BlockSpec((SUBLANE, MOE_BLOCK), lambda i: (0, i)),
                   pl.BlockSpec((SUBLANE, MOE_BLOCK), lambda i: (0, i)),
                   pl.BlockSpec((None, N_EXPERTS, LANE), lambda i: (i, 0, 0))],
        out_shape=[jax.ShapeDtypeStruct((SUBLANE, t), jnp.int32),
                   jax.ShapeDtypeStruct((SUBLANE, t), F32),
                   jax.ShapeDtypeStruct((nb, N_EXPERTS, LANE), F32)],
        compiler_params=_params("parallel"),
        name="router",
    )(x, wr_t)


def _dispatch_kernel(x_ref, ri_ref, o_ref):
    rows, tb = o_ref.shape[0], x_ref.shape[0]
    r = lax.broadcasted_iota(jnp.int32, (rows, tb), 0)
    sel = jnp.logical_or(r == ri_ref[2:3, :], r == ri_ref[3:4, :])
    perm = jnp.where(sel, 1.0, 0.0).astype(BF16)
    o_ref[...] = jnp.dot(perm, x_ref[...].astype(BF16), preferred_element_type=F32).astype(BF16)


def _dispatch(x, ri):
    t, d = x.shape
    nb = t // MOE_BLOCK
    return pl.pallas_call(
        _dispatch_kernel,
        grid=(nb,),
        in_specs=[pl.BlockSpec((MOE_BLOCK, d), lambda i: (i, 0)),
                  pl.BlockSpec((SUBLANE, MOE_BLOCK), lambda i: (0, i))],
        out_specs=pl.BlockSpec((LOC_ROWS, d), lambda i: (i, 0)),
        out_shape=jax.ShapeDtypeStruct((nb * LOC_ROWS, d), BF16),
        compiler_params=_params("parallel"),
        name="dispatch",
    )(x, ri)


def _experts_kernel(te_ref, nu_ref, gran_ref, x_hbm, wg_ref, wu_ref, wd_ref, yinit_hbm, y_hbm,
                    xbuf, ybuf, acc_ref, sem_in, sem_out):
    del te_ref, yinit_hbm
    i = pl.program_id(0)
    j = pl.program_id(1)
    nj = pl.num_programs(1)
    n_used = nu_ref[0]

    def for_granules(tile, fn):
        def body(s, carry):
            g = gran_ref[tile * GRAN_PER_TILE + s]

            @pl.when(g >= 0)
            def _():
                fn(pl.ds(pl.multiple_of(g * GRANULE, GRANULE), GRANULE),
                   pl.ds(pl.multiple_of(s * GRANULE, GRANULE), GRANULE))
            return carry
        lax.fori_loop(0, GRAN_PER_TILE, body, 0)

    def gather(tile, slot):
        return lambda src, dst: pltpu.make_async_copy(x_hbm.at[src], xbuf.at[slot, dst],
                                                      sem_in.at[slot])

    def scatter(src, dst):
        return pltpu.make_async_copy(ybuf.at[dst], y_hbm.at[src], sem_out.at[0])

    @pl.when(i < n_used)
    def _():
        slot = i % 2

        @pl.when(j == 0)
        def _():
            @pl.when(i == 0)
            def _():
                xbuf[...] = jnp.zeros(xbuf.shape, BF16)
                for_granules(0, lambda s, d: gather(0, 0)(s, d).start())

            for_granules(i, lambda s, d: gather(i, slot)(s, d).wait())

            @pl.when(i + 1 < n_used)
            def _():
                for_granules(i + 1, lambda s, d: gather(i + 1, 1 - slot)(s, d).start())

            acc_ref[...] = jnp.zeros(acc_ref.shape, F32)

        acc_ref[...] += _swiglu_step(xbuf[slot], wg_ref, wu_ref, wd_ref)

        @pl.when(j == nj - 1)
        def _():
            @pl.when(i > 0)
            def _():
                for_granules(i - 1, lambda s, d: scatter(s, d).wait())

            ybuf[...] = acc_ref[...].astype(BF16)
            for_granules(i, lambda s, d: scatter(s, d).start())

            @pl.when(i == n_used - 1)
            def _():
                for_granules(i, lambda s, d: scatter(s, d).wait())


def _experts(tile_expert, n_used, gran, x_loc, wg, wu, wd):
    rows, d = x_loc.shape
    ff = wg.shape[2]
    n_tiles = tile_expert.shape[0]
    nj = ff // MOE_FF_TILE

    def jj(i, j, nu):
        return jnp.where(i < nu[0], j, nj - 1)

    return pl.pallas_call(
        _experts_kernel,
        grid_spec=pltpu.PrefetchScalarGridSpec(
            num_scalar_prefetch=3,
            grid=(n_tiles, nj),
            in_specs=[pl.BlockSpec(memory_space=pl.ANY),
                      pl.BlockSpec((None, d, MOE_FF_TILE),
                                   lambda i, j, te, nu, gr: (te[i], 0, jj(i, j, nu))),
                      pl.BlockSpec((None, d, MOE_FF_TILE),
                                   lambda i, j, te, nu, gr: (te[i], 0, jj(i, j, nu))),
                      pl.BlockSpec((None, MOE_FF_TILE, d),
                                   lambda i, j, te, nu, gr: (te[i], jj(i, j, nu), 0)),
                      pl.BlockSpec(memory_space=pl.ANY)],
            out_specs=pl.BlockSpec(memory_space=pl.ANY),
            scratch_shapes=[pltpu.VMEM((2, FFN_ROW_TILE, d), BF16),
                            pltpu.VMEM((FFN_ROW_TILE, d), BF16),
                            pltpu.VMEM((FFN_ROW_TILE, d), F32),
                            pltpu.SemaphoreType.DMA((2,)),
                            pltpu.SemaphoreType.DMA((1,))]),
        out_shape=jax.ShapeDtypeStruct((rows, d), BF16),
        input_output_aliases={7: 0},
        compiler_params=_params("arbitrary", "arbitrary"),
        name="swiglu_experts",
    )(tile_expert, n_used, gran, x_loc, wg, wu, wd, jnp.zeros((rows, d), BF16))


def _combine_ln_kernel(y_ref, row_ref, w_ref, x_ref, g_ref, b_ref, o_ref):
    tb, rows = x_ref.shape[0], y_ref.shape[0]
    c = lax.broadcasted_iota(jnp.int32, (tb, rows), 1)
    q = (jnp.where(c == row_ref[:, 0:1], w_ref[:, 0:1], 0.0)
         + jnp.where(c == row_ref[:, 1:2], w_ref[:, 1:2], 0.0))
    f = jnp.dot(q.astype(BF16), y_ref[...], preferred_element_type=F32)
    o_ref[...] = _layer_norm(DEEPNORM_ALPHA * x_ref[...] + f, g_ref[...], b_ref[...])


def _combine_ln(y_loc, row_col, w_col, x, g, b):
    t, d = x.shape
    return pl.pallas_call(
        _combine_ln_kernel,
        grid=(t // MOE_BLOCK,),
        in_specs=[pl.BlockSpec((LOC_ROWS, d), lambda i: (i, 0)),
                  pl.BlockSpec((MOE_BLOCK, TOP_K), lambda i: (i, 0)),
                  pl.BlockSpec((MOE_BLOCK, TOP_K), lambda i: (i, 0)),
                  pl.BlockSpec((MOE_BLOCK, d), lambda i: (i, 0)),
                  pl.BlockSpec((1, d), lambda i: (0, 0)),
                  pl.BlockSpec((1, d), lambda i: (0, 0))],
        out_specs=pl.BlockSpec((MOE_BLOCK, d), lambda i: (i, 0)),
        out_shape=jax.ShapeDtypeStruct((t, d), F32),
        compiler_params=_params("parallel"),
        name="combine_ln",
    )(y_loc, row_col, w_col, x, g, b)


def _mixer(x, batch, seq, w_in, gm_ws, gm_bs, gm_ln_g, gm_ln_b, gla_wa2, gla_ba, gla_norm_g,
           w_out, ln_g, ln_b):
    w_gate = _fold_gate(w_in[:, PROJ_MAIN:], gla_wa2)
    w_ext = jnp.concatenate([w_in[:, :PROJ_MAIN], w_gate], axis=1).astype(BF16)
    proj, z = _inproj(x, w_ext)
    out_a = _gmlp(proj, gm_ws, gm_bs.reshape(GM_HEADS, GM_CHUNK, 1),
                  gm_ln_g.reshape(1, -1), gm_ln_b.reshape(1, -1))
    out_b = _gla(proj, z, gla_ba.reshape(1, -1), gla_norm_g.reshape(1, -1), batch, seq)
    return _outproj_ln(out_a, out_b, w_out.astype(BF16), x, ln_g.reshape(1, -1), ln_b.reshape(1, -1))


def _granule_table(counts):
    nb = counts.shape[0]
    n_tiles = (nb * (TOP_K * MOE_BLOCK + N_EXPERTS * (GRANULE - 1))) // FFN_ROW_TILE + N_EXPERTS
    gran_per_block = LOC_ROWS // GRANULE
    g = (counts + GRANULE - 1) // GRANULE
    seg_start = jnp.cumsum(g, axis=1) - g
    cum_incl = jnp.cumsum(g, axis=0)
    cum_excl = cum_incl - g
    total = cum_incl[-1]
    tiles_per_expert = (total + GRAN_PER_TILE - 1) // GRAN_PER_TILE
    tile_end = jnp.cumsum(tiles_per_expert)
    n_used = tile_end[-1:]
    tile = jnp.arange(n_tiles, dtype=jnp.int32)
    tile_expert = jnp.minimum(jnp.sum(tile[:, None] >= tile_end[None, :], axis=1), N_EXPERTS - 1)
    tile_in_expert = tile - (tile_end - tiles_per_expert)[tile_expert]
    vg = tile_in_expert[:, None] * GRAN_PER_TILE + jnp.arange(GRAN_PER_TILE, dtype=jnp.int32)[None, :]
    ci = cum_incl.T[tile_expert]
    blk = jnp.sum(vg[:, :, None] >= ci[:, None, :], axis=2)
    blk = jnp.minimum(blk, nb - 1)
    te2 = jnp.broadcast_to(tile_expert[:, None], blk.shape)
    phys = blk * gran_per_block + seg_start[blk, te2] + (vg - cum_excl[blk, te2])
    valid = jnp.logical_and(vg < total[tile_expert][:, None], (tile < n_used[0])[:, None])
    gran = jnp.where(valid, phys, -1).astype(jnp.int32).reshape(-1)
    return tile_expert.astype(jnp.int32), n_used.astype(jnp.int32), gran


def _moe_ffn(x, w_router, e_gate, e_up, e_down, ln_g, ln_b):
    ri, rw, cnt = _router(x, w_router.T)
    tile_expert, n_used, gran = _granule_table(cnt[:, :, 0].astype(jnp.int32))
    x_loc = _dispatch(x, ri)
    y_loc = _experts(tile_expert, n_used, gran, x_loc,
                     e_gate.astype(BF16), e_up.astype(BF16), e_down.astype(BF16))
    return _combine_ln(y_loc, ri[2:2 + TOP_K].T, rw[:TOP_K].T, x,
                       ln_g.reshape(1, -1), ln_b.reshape(1, -1))


def kernel(x, w_in, gm_ws, gm_bs, gm_ln_g, gm_ln_b, gla_wa2, gla_ba, gla_norm_g, w_out, ln_mix_g, ln_mix_b, ffn_w_gate, ffn_w_up, ffn_w_down, router_w, exp_w_gate, exp_w_up, exp_w_down, ln_ffn_g, ln_ffn_b):
    batch, seq, d = x.shape
    h = x.reshape(batch * seq, d)
    for layer in range(DEPTH):
        h = _mixer(h, batch, seq, w_in[layer], gm_ws[layer], gm_bs[layer], gm_ln_g[layer],
                   gm_ln_b[layer], gla_wa2[layer], gla_ba[layer], gla_norm_g[layer], w_out[layer],
                   ln_mix_g[layer], ln_mix_b[layer])
        i = layer // 2
        if layer % 2 == 0:
            h = _swiglu_ln(h, ffn_w_gate[i].astype(BF16), ffn_w_up[i].astype(BF16),
                           ffn_w_down[i].astype(BF16), ln_ffn_g[layer].reshape(1, -1),
                           ln_ffn_b[layer].reshape(1, -1))
        else:
            h = _moe_ffn(h, router_w[i], exp_w_gate[i], exp_w_up[i], exp_w_down[i],
                         ln_ffn_g[layer], ln_ffn_b[layer])
    return h.reshape(batch, seq, d)
```

```python
import functools

import jax
import jax.numpy as jnp
from jax import lax
from jax.experimental import pallas as pl
from jax.experimental.pallas import tpu as pltpu

F32 = jnp.float32
BF16 = jnp.bfloat16

D_MODEL = 1024
DEPTH = 2
GM_HEADS = 4
GM_WIDTH = 512
GM_HEAD_DIM = 128
GM_CHUNK = 128
GLA_HEADS = 4
GLA_WIDTH = 512
GLA_DV = 128
GLA_KEY_WIDTH = 256
GLA_DK = 64
GLA_GATE_RANK = 16
GLA_TAU = 16.0
GLA_CHUNK = 64
PROJ_MAIN = 2 * GM_WIDTH + 2 * GLA_KEY_WIDTH + 2 * GLA_WIDTH
N_EXPERTS = 8
TOP_K = 2
LN_EPS = 1e-5
RMS_EPS = 1e-6
DEEPNORM_ALPHA = (2 * DEPTH) ** 0.25

V7X_VMEM_LIMIT_BYTES = 56 * 1024 * 1024
LANE = 128
SUBLANE = 8
MXU_N = 256

ROW_TILE = 512
FFN_ROW_TILE = 1024
DENSE_FF_TILE = 256
MOE_FF_TILE = 896
MOE_BLOCK = 512
GRANULE = 16
LOC_ROWS = -(-(TOP_K * MOE_BLOCK + N_EXPERTS * (GRANULE - 1)) // LANE) * LANE
GRAN_PER_TILE = FFN_ROW_TILE // GRANULE
GLA_UNROLL = 8


def _params(*sem):
    return pltpu.CompilerParams(dimension_semantics=sem, vmem_limit_bytes=V7X_VMEM_LIMIT_BYTES)


def _gelu(x):
    return 0.5 * x * (1.0 + lax.erf(x * (2.0 ** -0.5)))


def _layer_norm(y, g, b):
    mu = jnp.mean(y, axis=-1, keepdims=True)
    d = y - mu
    var = jnp.mean(d * d, axis=-1, keepdims=True)
    return d * lax.rsqrt(var + LN_EPS) * g + b


def _fold_kernel(w1_ref, w2_ref, o_ref):
    acc = jnp.zeros(o_ref.shape, F32)
    for r in range(GLA_GATE_RANK):
        acc = acc + w1_ref[:, r:r + 1] * w2_ref[r:r + 1, :]
    o_ref[...] = acc


def _fold_gate(w1, w2):
    return pl.pallas_call(
        _fold_kernel,
        out_shape=jax.ShapeDtypeStruct((w1.shape[0], w2.shape[1]), F32),
        name="fold_gate",
    )(w1, w2)


def _inproj_kernel(x_ref, w_ref, p_ref, z_ref):
    xb = x_ref[...].astype(BF16)
    for n in range(PROJ_MAIN // MXU_N):
        cols = slice(n * MXU_N, (n + 1) * MXU_N)
        p_ref[:, cols] = jnp.dot(xb, w_ref[:, cols], preferred_element_type=F32).astype(BF16)
    z_ref[...] = jnp.dot(xb, w_ref[:, PROJ_MAIN:], preferred_element_type=F32)


def _inproj(x, w):
    t, d = x.shape
    n = w.shape[1]
    return pl.pallas_call(
        _inproj_kernel,
        grid=(t // ROW_TILE,),
        in_specs=[pl.BlockSpec((ROW_TILE, d), lambda i: (i, 0)),
                  pl.BlockSpec((d, n), lambda i: (0, 0))],
        out_specs=[pl.BlockSpec((ROW_TILE, PROJ_MAIN), lambda i: (i, 0)),
                   pl.BlockSpec((ROW_TILE, GLA_KEY_WIDTH), lambda i: (i, 0))],
        out_shape=[jax.ShapeDtypeStruct((t, PROJ_MAIN), BF16),
                   jax.ShapeDtypeStruct((t, GLA_KEY_WIDTH), F32)],
        compiler_params=_params("parallel"),
        name="inproj",
    )(x, w)


def _gmlp_kernel(uv_ref, w_ref, bcol_ref, g_ref, b_ref, o_ref):
    rows = uv_ref.shape[0]
    r = lax.broadcasted_iota(jnp.int32, (GM_CHUNK, GM_CHUNK), 0)
    c = lax.broadcasted_iota(jnp.int32, (GM_CHUNK, GM_CHUNK), 1)
    causal = r >= c
    for h in range(GM_HEADS):
        cols = slice(h * GM_HEAD_DIM, (h + 1) * GM_HEAD_DIM)
        vcols = slice(GM_WIDTH + h * GM_HEAD_DIM, GM_WIDTH + (h + 1) * GM_HEAD_DIM)
        wm = jnp.where(causal, w_ref[h], 0.0).astype(BF16)
        v = _gelu(uv_ref[:, vcols].astype(F32))
        vn = _layer_norm(v, g_ref[:, cols], b_ref[:, cols]).astype(BF16)
        u = _gelu(uv_ref[:, cols].astype(F32))
        for ci in range(rows // GM_CHUNK):
            rs = slice(ci * GM_CHUNK, (ci + 1) * GM_CHUNK)
            s = jnp.dot(wm, vn[rs], preferred_element_type=F32) + bcol_ref[h]
            o_ref[rs, cols] = (u[rs] * s).astype(BF16)


def _gmlp(proj, w_s, b_col, ln_g, ln_b):
    t = proj.shape[0]
    return pl.pallas_call(
        _gmlp_kernel,
        grid=(t // ROW_TILE,),
        in_specs=[pl.BlockSpec((ROW_TILE, 2 * GM_WIDTH), lambda i: (i, 0)),
                  pl.BlockSpec((GM_HEADS, GM_CHUNK, GM_CHUNK), lambda i: (0, 0, 0)),
                  pl.BlockSpec((GM_HEADS, GM_CHUNK, 1), lambda i: (0, 0, 0)),
                  pl.BlockSpec((1, GM_WIDTH), lambda i: (0, 0)),
                  pl.BlockSpec((1, GM_WIDTH), lambda i: (0, 0))],
        out_specs=pl.BlockSpec((ROW_TILE, GM_WIDTH), lambda i: (i, 0)),
        out_shape=jax.ShapeDtypeStruct((t, GM_WIDTH), BF16),
        compiler_params=_params("parallel"),
        name="gmlp",
    )(proj, w_s, b_col, ln_g, ln_b)


def _gla_kernel(q_ref, k_ref, v_ref, g_ref, z_ref, ba_ref, ng_ref, o_ref, st_ref):
    rows = q_ref.shape[0]

    @pl.when(pl.program_id(1) == 0)
    def _():
        st_ref[...] = jnp.zeros(st_ref.shape, F32)

    r = lax.broadcasted_iota(jnp.int32, (GLA_CHUNK, GLA_CHUNK), 0)
    c = lax.broadcasted_iota(jnp.int32, (GLA_CHUNK, GLA_CHUNK), 1)
    causal = r >= c
    ones_tril = jnp.where(causal, 1.0, 0.0).astype(BF16)
    nt = (((1,), (1,)), ((), ()))
    tn = (((0,), (0,)), ((), ()))

    def chunk(ci, carry):
        rs = pl.ds(pl.multiple_of(ci * GLA_CHUNK, GLA_CHUNK), GLA_CHUNK)
        z = z_ref[rs, :] + ba_ref[...]
        la = (jnp.minimum(z, 0.0) - jnp.log1p(jnp.exp(-jnp.abs(z)))) * (1.0 / GLA_TAU)
        la_hi = la.astype(BF16)
        la_lo = (la - la_hi.astype(F32)).astype(BF16)
        b = (jnp.dot(ones_tril, la_hi, preferred_element_type=F32)
             + jnp.dot(ones_tril, la_lo, preferred_element_type=F32))
        b_last = b[GLA_CHUNK - 1:GLA_CHUNK, :]
        q = q_ref[rs, :].astype(F32) * (GLA_DK ** -0.5)
        k = k_ref[rs, :].astype(F32)
        q_dec = (q * jnp.exp(b)).astype(BF16)
        k_inv = (k * jnp.exp(-b)).astype(BF16)
        k_dec = (k * jnp.exp(b_last - b)).astype(BF16)
        decay = jnp.exp(b_last)
        for h in range(GLA_HEADS):
            ks = slice(h * GLA_DK, (h + 1) * GLA_DK)
            vs = slice(h * GLA_DV, (h + 1) * GLA_DV)
            scores = lax.dot_general(q_dec[:, ks], k_inv[:, ks], nt, preferred_element_type=F32)
            scores = jnp.where(causal, scores, 0.0).astype(BF16)
            vh = v_ref[rs, vs]
            state_t = st_ref[h]
            o = (jnp.dot(scores, vh, preferred_element_type=F32)
                 + lax.dot_general(q_dec[:, ks], state_t.astype(BF16), nt, preferred_element_type=F32))
            st_ref[h] = (state_t * decay[:, ks]
                         + lax.dot_general(vh, k_dec[:, ks], tn, preferred_element_type=F32))
            o = o * lax.rsqrt(jnp.mean(o * o, axis=-1, keepdims=True) + RMS_EPS) * ng_ref[:, vs]
            gate = g_ref[rs, vs].astype(F32)
            o_ref[rs, vs] = (o * (gate * jax.nn.sigmoid(gate))).astype(BF16)
        return carry

    lax.fori_loop(0, rows // GLA_CHUNK, chunk, 0, unroll=GLA_UNROLL)


def _gla(proj, z, ba, norm_g, batch, seq):
    t = proj.shape[0]
    nseq = seq // ROW_TILE
    kw, vw = GLA_KEY_WIDTH, GLA_WIDTH
    q_blk = (2 * GM_WIDTH) // kw
    v_blk = (2 * GM_WIDTH + 2 * kw) // vw
    row = lambda b, s: b * nseq + s
    return pl.pallas_call(
        _gla_kernel,
        grid=(batch, nseq),
        in_specs=[pl.BlockSpec((ROW_TILE, kw), lambda b, s: (row(b, s), q_blk)),
                  pl.BlockSpec((ROW_TILE, kw), lambda b, s: (row(b, s), q_blk + 1)),
                  pl.BlockSpec((ROW_TILE, vw), lambda b, s: (row(b, s), v_blk)),
                  pl.BlockSpec((ROW_TILE, vw), lambda b, s: (row(b, s), v_blk + 1)),
                  pl.BlockSpec((ROW_TILE, kw), lambda b, s: (row(b, s), 0)),
                  pl.BlockSpec((1, kw), lambda b, s: (0, 0)),
                  pl.BlockSpec((1, vw), lambda b, s: (0, 0))],
        out_specs=pl.BlockSpec((ROW_TILE, vw), lambda b, s: (row(b, s), 0)),
        out_shape=jax.ShapeDtypeStruct((t, vw), BF16),
        scratch_shapes=[pltpu.VMEM((GLA_HEADS, GLA_DV, GLA_DK), F32)],
        compiler_params=_params("parallel", "arbitrary"),
        name="gla",
    )(proj, proj, proj, proj, z, ba, norm_g)


def _outproj_ln_kernel(a_ref, b_ref, w_ref, x_ref, g_ref, be_ref, o_ref):
    h = (jnp.dot(a_ref[...], w_ref[:GM_WIDTH, :], preferred_element_type=F32)
         + jnp.dot(b_ref[...], w_ref[GM_WIDTH:, :], preferred_element_type=F32))
    o_ref[...] = _layer_norm(DEEPNORM_ALPHA * x_ref[...] + h, g_ref[...], be_ref[...])


def _outproj_ln(out_a, out_b, w, x, g, b):
    t, d = x.shape
    return pl.pallas_call(
        _outproj_ln_kernel,
        grid=(t // ROW_TILE,),
        in_specs=[pl.BlockSpec((ROW_TILE, GM_WIDTH), lambda i: (i, 0)),
                  pl.BlockSpec((ROW_TILE, GLA_WIDTH), lambda i: (i, 0)),
                  pl.BlockSpec((GM_WIDTH + GLA_WIDTH, d), lambda i: (0, 0)),
                  pl.BlockSpec((ROW_TILE, d), lambda i: (i, 0)),
                  pl.BlockSpec((1, d), lambda i: (0, 0)),
                  pl.BlockSpec((1, d), lambda i: (0, 0))],
        out_specs=pl.BlockSpec((ROW_TILE, d), lambda i: (i, 0)),
        out_shape=jax.ShapeDtypeStruct((t, d), F32),
        compiler_params=_params("parallel"),
        name="outproj_ln",
    )(out_a, out_b, w, x, g, b)


def _swiglu_step(xb, wg_ref, wu_ref, wd_ref):
    gate = jnp.dot(xb, wg_ref[...], preferred_element_type=F32)
    up = jnp.dot(xb, wu_ref[...], preferred_element_type=F32)
    hid = (gate * jax.nn.sigmoid(gate) * up).astype(BF16)
    return jnp.dot(hid, wd_ref[...], preferred_element_type=F32)


def _swiglu_ln_kernel(x_ref, wg_ref, wu_ref, wd_ref, g_ref, b_ref, o_ref):
    xb = x_ref[...].astype(BF16)
    acc = jnp.zeros(x_ref.shape, F32)
    for n in range(wg_ref.shape[1] // DENSE_FF_TILE):
        cols = slice(n * DENSE_FF_TILE, (n + 1) * DENSE_FF_TILE)
        gate = jnp.dot(xb, wg_ref[:, cols], preferred_element_type=F32)
        up = jnp.dot(xb, wu_ref[:, cols], preferred_element_type=F32)
        hid = (gate * jax.nn.sigmoid(gate) * up).astype(BF16)
        acc = acc + jnp.dot(hid, wd_ref[cols, :], preferred_element_type=F32)
    o_ref[...] = _layer_norm(DEEPNORM_ALPHA * x_ref[...] + acc, g_ref[...], b_ref[...])


def _swiglu_ln(x, wg, wu, wd, g, b):
    rows, d = x.shape
    ff = wg.shape[1]
    resident = pl.Buffered(1)
    return pl.pallas_call(
        _swiglu_ln_kernel,
        grid=(rows // ROW_TILE,),
        in_specs=[pl.BlockSpec((ROW_TILE, d), lambda i: (i, 0)),
                  pl.BlockSpec((d, ff), lambda i: (0, 0), pipeline_mode=resident),
                  pl.BlockSpec((d, ff), lambda i: (0, 0), pipeline_mode=resident),
                  pl.BlockSpec((ff, d), lambda i: (0, 0), pipeline_mode=resident),
                  pl.BlockSpec((1, d), lambda i: (0, 0)),
                  pl.BlockSpec((1, d), lambda i: (0, 0))],
        out_specs=pl.BlockSpec((ROW_TILE, d), lambda i: (i, 0)),
        out_shape=jax.ShapeDtypeStruct((rows, d), F32),
        compiler_params=_params("parallel"),
        name="swiglu_ln",
    )(x, wg, wu, wd, g, b)


def _router_kernel(x_ref, wr_ref, ri_ref, rw_ref, cnt_ref):
    tb = x_ref.shape[0]
    logits = lax.dot_general(wr_ref[...], x_ref[...], (((1,), (1,)), ((), ())),
                             precision=lax.Precision.HIGHEST, preferred_element_type=F32)
    e_iota = lax.broadcasted_iota(jnp.int32, (N_EXPERTS, tb), 0)
    m0 = jnp.max(logits, axis=0, keepdims=True)
    i0 = jnp.min(jnp.where(logits == m0, e_iota, N_EXPERTS), axis=0, keepdims=True)
    rest = jnp.where(e_iota == i0, -jnp.inf, logits)
    m1 = jnp.max(rest, axis=0, keepdims=True)
    i1 = jnp.min(jnp.where(rest == m1, e_iota, N_EXPERTS), axis=0, keepdims=True)
    ex = jnp.exp(m1 - m0)
    w0 = 1.0 / (1.0 + ex)
    w1 = ex / (1.0 + ex)
    oh0 = e_iota == i0
    oh1 = e_iota == i1
    onehot = jnp.where(jnp.logical_or(oh0, oh1), 1.0, 0.0)
    tr = lax.broadcasted_iota(jnp.int32, (tb, tb), 0)
    tc = lax.broadcasted_iota(jnp.int32, (tb, tb), 1)
    before = jnp.where(tr < tc, 1.0, 0.0).astype(BF16)
    rank = jnp.dot(onehot.astype(BF16), before, preferred_element_type=F32)
    count = jnp.broadcast_to(jnp.sum(onehot, axis=1, keepdims=True), (N_EXPERTS, LANE))
    cap = jnp.ceil(count * (1.0 / GRANULE)) * GRANULE
    e_sub = lax.broadcasted_iota(jnp.int32, (N_EXPERTS, LANE), 0)
    seg = jnp.zeros((N_EXPERTS, LANE), F32)
    for e in range(N_EXPERTS - 1):
        seg = seg + jnp.where(e_sub > e, cap[e:e + 1, :], 0.0)
    row = rank + seg[:, 0:1]
    row0 = jnp.sum(jnp.where(oh0, row, 0.0), axis=0, keepdims=True)
    row1 = jnp.sum(jnp.where(oh1, row, 0.0), axis=0, keepdims=True)
    cnt_ref[...] = count
    ri_ref[...] = jnp.zeros(ri_ref.shape, jnp.int32)
    ri_ref[0:1, :] = i0
    ri_ref[1:2, :] = i1
    ri_ref[2:3, :] = row0.astype(jnp.int32)
    ri_ref[3:4, :] = row1.astype(jnp.int32)
    rw_ref[...] = jnp.zeros(rw_ref.shape, F32)
    rw_ref[0:1, :] = w0
    rw_ref[1:2, :] = w1


def _route_dispatch_kernel(x_ref, wr_ref, ri_ref, rw_ref, cnt_ref, o_ref):
    _router_kernel(x_ref, wr_ref, ri_ref, rw_ref, cnt_ref)
    rows, tb = o_ref.shape[0], x_ref.shape[0]
    r = lax.broadcasted_iota(jnp.int32, (rows, tb), 0)
    sel = jnp.logical_or(r == ri_ref[2:3, :], r == ri_ref[3:4, :])
    perm = jnp.where(sel, 1.0, 0.0).astype(BF16)
    o_ref[...] = jnp.dot(perm, x_ref[...].astype(BF16), preferred_element_type=F32).astype(BF16)


def _route_dispatch(x, wr_t):
    t, d = x.shape
    nb = t // MOE_BLOCK
    return pl.pallas_call(
        _route_dispatch_kernel,
        grid=(nb,),
        in_specs=[pl.BlockSpec((MOE_BLOCK, d), lambda i: (i, 0)),
                  pl.BlockSpec((N_EXPERTS, d), lambda i: (0, 0))],
        out_specs=[pl.BlockSpec((SUBLANE, MOE_BLOCK), lambda i: (0, i)),
                   pl.BlockSpec((SUBLANE, MOE_BLOCK), lambda i: (0, i)),
                   pl.BlockSpec((None, N_EXPERTS, LANE), lambda i: (i, 0, 0)),
                   pl.BlockSpec((LOC_ROWS, d), lambda i: (i, 0))],
        out_shape=[jax.ShapeDtypeStruct((SUBLANE, t), jnp.int32),
                   jax.ShapeDtypeStruct((SUBLANE, t), F32),
                   jax.ShapeDtypeStruct((nb, N_EXPERTS, LANE), F32),
                   jax.ShapeDtypeStruct((nb * LOC_ROWS, d), BF16)],
        compiler_params=_params("parallel"),
        name="route_dispatch",
    )(x, wr_t)


def _experts_kernel(te_ref, nu_ref, gran_ref, x_hbm, wg_ref, wu_ref, wd_ref, yinit_hbm, y_hbm,
                    xbuf, ybuf, acc_ref, sem_in, sem_out):
    del te_ref, yinit_hbm
    i = pl.program_id(0)
    j = pl.program_id(1)
    nj = pl.num_programs(1)
    n_used = nu_ref[0]

    def for_granules(tile, fn):
        def body(s, carry):
            g = gran_ref[tile * GRAN_PER_TILE + s]

            @pl.when(g >= 0)
            def _():
                fn(pl.ds(pl.multiple_of(g * GRANULE, GRANULE), GRANULE),
                   pl.ds(pl.multiple_of(s * GRANULE, GRANULE), GRANULE))
            return carry
        lax.fori_loop(0, GRAN_PER_TILE, body, 0)

    def gather(tile, slot):
        return lambda src, dst: pltpu.make_async_copy(x_hbm.at[src], xbuf.at[slot, dst],
                                                      sem_in.at[slot])

    def scatter(src, dst):
        return pltpu.make_async_copy(ybuf.at[dst], y_hbm.at[src], sem_out.at[0])

    @pl.when(i < n_used)
    def _():
        slot = i % 2

        @pl.when(j == 0)
        def _():
            @pl.when(i == 0)
            def _():
                xbuf[...] = jnp.zeros(xbuf.shape, BF16)
                for_granules(0, lambda s, d: gather(0, 0)(s, d).start())

            for_granules(i, lambda s, d: gather(i, slot)(s, d).wait())

            @pl.when(i + 1 < n_used)
            def _():
                for_granules(i + 1, lambda s, d: gather(i + 1, 1 - slot)(s, d).start())

            acc_ref[...] = jnp.zeros(acc_ref.shape, F32)

        acc_ref[...] += _swiglu_step(xbuf[slot], wg_ref, wu_ref, wd_ref)

        @pl.when(j == nj - 1)
        def _():
            @pl.when(i > 0)
            def _():
                for_granules(i - 1, lambda s, d: scatter(s, d).wait())

            ybuf[...] = acc_ref[...].astype(BF16)
            for_granules(i, lambda s, d: scatter(s, d).start())

            @pl.when(i == n_used - 1)
            def _():
                for_granules(i, lambda s, d: scatter(s, d).wait())


def _experts(tile_expert, n_used, gran, x_loc, wg, wu, wd):
    rows, d = x_loc.shape
    ff = wg.shape[2]
    n_tiles = tile_expert.shape[0]
    nj = ff // MOE_FF_TILE

    def jj(i, j, nu):
        return jnp.where(i < nu[0], j, nj - 1)

    return pl.pallas_call(
        _experts_kernel,
        grid_spec=pltpu.PrefetchScalarGridSpec(
            num_scalar_prefetch=3,
            grid=(n_tiles, nj),
            in_specs=[pl.BlockSpec(memory_space=pl.ANY),
                      pl.BlockSpec((None, d, MOE_FF_TILE),
                                   lambda i, j, te, nu, gr: (te[i], 0, jj(i, j, nu))),
                      pl.BlockSpec((None, d, MOE_FF_TILE),
                                   lambda i, j, te, nu, gr: (te[i], 0, jj(i, j, nu))),
                      pl.BlockSpec((None, MOE_FF_TILE, d),
                                   lambda i, j, te, nu, gr: (te[i], jj(i, j, nu), 0)),
                      pl.BlockSpec(memory_space=pl.ANY)],
            out_specs=pl.BlockSpec(memory_space=pl.ANY),
            scratch_shapes=[pltpu.VMEM((2, FFN_ROW_TILE, d), BF16),
                            pltpu.VMEM((FFN_ROW_TILE, d), BF16),
                            pltpu.VMEM((FFN_ROW_TILE, d), F32),
                            pltpu.SemaphoreType.DMA((2,)),
                            pltpu.SemaphoreType.DMA((1,))]),
        out_shape=jax.ShapeDtypeStruct((rows, d), BF16),
        input_output_aliases={7: 0},
        compiler_params=_params("arbitrary", "arbitrary"),
        name="swiglu_experts",
    )(tile_expert, n_used, gran, x_loc, wg, wu, wd, jnp.zeros((rows, d), BF16))


def _combine_ln_kernel(y_ref, row_ref, w_ref, x_ref, g_ref, b_ref, o_ref):
    tb, rows = x_ref.shape[0], y_ref.shape[0]
    c = lax.broadcasted_iota(jnp.int32, (tb, rows), 1)
    q = (jnp.where(c == row_ref[:, 0:1], w_ref[:, 0:1], 0.0)
         + jnp.where(c == row_ref[:, 1:2], w_ref[:, 1:2], 0.0))
    f = jnp.dot(q.astype(BF16), y_ref[...], preferred_element_type=F32)
    o_ref[...] = _layer_norm(DEEPNORM_ALPHA * x_ref[...] + f, g_ref[...], b_ref[...])


def _combine_ln(y_loc, row_col, w_col, x, g, b):
    t, d = x.shape
    return pl.pallas_call(
        _combine_ln_kernel,
        grid=(t // MOE_BLOCK,),
        in_specs=[pl.BlockSpec((LOC_ROWS, d), lambda i: (i, 0)),
                  pl.BlockSpec((MOE_BLOCK, TOP_K), lambda i: (i, 0)),
                  pl.BlockSpec((MOE_BLOCK, TOP_K), lambda i: (i, 0)),
                  pl.BlockSpec((MOE_BLOCK, d), lambda i: (i, 0)),
                  pl.BlockSpec((1, d), lambda i: (0, 0)),
                  pl.BlockSpec((1, d), lambda i: (0, 0))],
        out_specs=pl.BlockSpec((MOE_BLOCK, d), lambda i: (i, 0)),
        out_shape=jax.ShapeDtypeStruct((t, d), F32),
        compiler_params=_params("parallel"),
        name="combine_ln",
    )(y_loc, row_col, w_col, x, g, b)


def _mixer(x, batch, seq, w_in, gm_ws, gm_bs, gm_ln_g, gm_ln_b, gla_wa2, gla_ba, gla_norm_g,
           w_out, ln_g, ln_b):
    w_gate = _fold_gate(w_in[:, PROJ_MAIN:], gla_wa2)
    w_ext = jnp.concatenate([w_in[:, :PROJ_MAIN], w_gate], axis=1).astype(BF16)
    proj, z = _inproj(x, w_ext)
    out_a = _gmlp(proj, gm_ws, gm_bs.reshape(GM_HEADS, GM_CHUNK, 1),
                  gm_ln_g.reshape(1, -1), gm_ln_b.reshape(1, -1))
    out_b = _gla(proj, z, gla_ba.reshape(1, -1), gla_norm_g.reshape(1, -1), batch, seq)
    return _outproj_ln(out_a, out_b, w_out.astype(BF16), x, ln_g.reshape(1, -1), ln_b.reshape(1, -1))


def _granule_table(counts):
    nb = counts.shape[0]
    n_tiles = (nb * (TOP_K * MOE_BLOCK + N_EXPERTS * (GRANULE - 1))) // FFN_ROW_TILE + N_EXPERTS
    gran_per_block = LOC_ROWS // GRANULE
    g = (counts + GRANULE - 1) // GRANULE
    seg_start = jnp.cumsum(g, axis=1) - g
    cum_incl = jnp.cumsum(g, axis=0)
    cum_excl = cum_incl - g
    total = cum_incl[-1]
    tiles_per_expert = (total + GRAN_PER_TILE - 1) // GRAN_PER_TILE
    tile_end = jnp.cumsum(tiles_per_expert)
    n_used = tile_end[-1:]
    tile = jnp.arange(n_tiles, dtype=jnp.int32)
    tile_expert = jnp.minimum(jnp.sum(tile[:, None] >= tile_end[None, :], axis=1), N_EXPERTS - 1)
    pick = (tile_expert[:, None] == jnp.arange(N_EXPERTS, dtype=jnp.int32)[None, :]).astype(jnp.int32)
    per_tile = lambda tab: jnp.sum(pick[:, :, None] * tab.T[None, :, :], axis=1)
    first_tile = jnp.sum(pick * (tile_end - tiles_per_expert)[None, :], axis=1)
    vg = ((tile - first_tile)[:, None] * GRAN_PER_TILE
          + jnp.arange(GRAN_PER_TILE, dtype=jnp.int32)[None, :])[:, :, None]
    lo = per_tile(cum_excl)[:, None, :]
    hi = per_tile(cum_incl)[:, None, :]
    base = (jnp.arange(nb, dtype=jnp.int32)[None, :] * gran_per_block + per_tile(seg_start))[:, None, :]
    inside = jnp.logical_and(vg >= lo, vg < hi)
    phys = jnp.sum(jnp.where(inside, base + vg - lo, 0), axis=2)
    valid = jnp.logical_and(jnp.any(inside, axis=2), (tile < n_used[0])[:, None])
    gran = jnp.where(valid, phys, -1).astype(jnp.int32).reshape(-1)
    return tile_expert.astype(jnp.int32), n_used.astype(jnp.int32), gran


def _moe_ffn(x, w_router, e_gate, e_up, e_down, ln_g, ln_b):
    ri, rw, cnt, x_loc = _route_dispatch(x, w_router.T)
    tile_expert, n_used, gran = _granule_table(cnt[:, :, 0].astype(jnp.int32))
    y_loc = _experts(tile_expert, n_used, gran, x_loc,
                     e_gate.astype(BF16), e_up.astype(BF16), e_down.astype(BF16))
    return _combine_ln(y_loc, ri[2:2 + TOP_K].T, rw[:TOP_K].T, x,
                       ln_g.reshape(1, -1), ln_b.reshape(1, -1))


def kernel(x, w_in, gm_ws, gm_bs, gm_ln_g, gm_ln_b, gla_wa2, gla_ba, gla_norm_g, w_out, ln_mix_g, ln_mix_b, ffn_w_gate, ffn_w_up, ffn_w_down, router_w, exp_w_gate, exp_w_up, exp_w_down, ln_ffn_g, ln_ffn_b):
    batch, seq, d = x.shape
    h = x.reshape(batch * seq, d)
    for layer in range(DEPTH):
        h = _mixer(h, batch, seq, w_in[layer], gm_ws[layer], gm_bs[layer], gm_ln_g[layer],
                   gm_ln_b[layer], gla_wa2[layer], gla_ba[layer], gla_norm_g[layer], w_out[layer],
                   ln_mix_g[layer], ln_mix_b[layer])
        i = layer // 2
        if layer % 2 == 0:
            h = _swiglu_ln(h, ffn_w_gate[i].astype(BF16), ffn_w_up[i].astype(BF16),
                           ffn_w_down[i].astype(BF16), ln_ffn_g[layer].reshape(1, -1),
                           ln_ffn_b[layer].reshape(1, -1))
        else:
            h = _moe_ffn(h, router_w[i], exp_w_gate[i], exp_w_up[i], exp_w_down[i],
                         ln_ffn_g[layer], ln_ffn_b[layer])
    return h.reshape(batch, seq, d)
```

```python
import functools

import jax
import jax.numpy as jnp
from jax import lax
from jax.experimental import pallas as pl
from jax.experimental.pallas import tpu as pltpu

F32 = jnp.float32
BF16 = jnp.bfloat16

D_MODEL = 1024
DEPTH = 2
GM_HEADS = 4
GM_WIDTH = 512
GM_HEAD_DIM = 128
GM_CHUNK = 128
GLA_HEADS = 4
GLA_WIDTH = 512
GLA_DV = 128
GLA_KEY_WIDTH = 256
GLA_DK = 64
GLA_GATE_RANK = 16
GLA_TAU = 16.0
GLA_CHUNK = 64
PROJ_MAIN = 2 * GM_WIDTH + 2 * GLA_KEY_WIDTH + 2 * GLA_WIDTH
N_EXPERTS = 8
TOP_K = 2
LN_EPS = 1e-5
RMS_EPS = 1e-6
DEEPNORM_ALPHA = (2 * DEPTH) ** 0.25

V7X_VMEM_LIMIT_BYTES = 56 * 1024 * 1024
LANE = 128
SUBLANE = 8
MXU_N = 256

ROW_TILE = 512
FFN_ROW_TILE = 1024
DENSE_FF_TILE = 256
MOE_FF_TILE = 512
MOE_BLOCK = 512
GRANULE = 16
LOC_ROWS = -(-(TOP_K * MOE_BLOCK + N_EXPERTS * (GRANULE - 1)) // LANE) * LANE
GRAN_PER_TILE = FFN_ROW_TILE // GRANULE
GLA_UNROLL = 8


def _params(*sem, **kw):
    return pltpu.CompilerParams(dimension_semantics=sem, vmem_limit_bytes=V7X_VMEM_LIMIT_BYTES, **kw)


def _gelu(x):
    return 0.5 * x * (1.0 + lax.erf(x * (2.0 ** -0.5)))


def _layer_norm(y, g, b):
    mu = jnp.mean(y, axis=-1, keepdims=True)
    d = y - mu
    var = jnp.mean(d * d, axis=-1, keepdims=True)
    return d * lax.rsqrt(var + LN_EPS) * g + b


def _fold_kernel(w1_ref, w2_ref, o_ref):
    acc = jnp.zeros(o_ref.shape, F32)
    for r in range(GLA_GATE_RANK):
        acc = acc + w1_ref[:, r:r + 1] * w2_ref[r:r + 1, :]
    o_ref[...] = acc


def _fold_gate(w1, w2):
    return pl.pallas_call(
        _fold_kernel,
        out_shape=jax.ShapeDtypeStruct((w1.shape[0], w2.shape[1]), F32),
        name="fold_gate",
    )(w1, w2)


def _inproj_kernel(x_ref, w_ref, p_ref, z_ref):
    xb = x_ref[...].astype(BF16)
    for n in range(PROJ_MAIN // MXU_N):
        cols = slice(n * MXU_N, (n + 1) * MXU_N)
        p_ref[:, cols] = jnp.dot(xb, w_ref[:, cols], preferred_element_type=F32).astype(BF16)
    z_ref[...] = jnp.dot(xb, w_ref[:, PROJ_MAIN:], preferred_element_type=F32)


def _inproj(x, w):
    t, d = x.shape
    n = w.shape[1]
    return pl.pallas_call(
        _inproj_kernel,
        grid=(t // ROW_TILE,),
        in_specs=[pl.BlockSpec((ROW_TILE, d), lambda i: (i, 0)),
                  pl.BlockSpec((d, n), lambda i: (0, 0))],
        out_specs=[pl.BlockSpec((ROW_TILE, PROJ_MAIN), lambda i: (i, 0)),
                   pl.BlockSpec((ROW_TILE, GLA_KEY_WIDTH), lambda i: (i, 0))],
        out_shape=[jax.ShapeDtypeStruct((t, PROJ_MAIN), BF16),
                   jax.ShapeDtypeStruct((t, GLA_KEY_WIDTH), F32)],
        compiler_params=_params("parallel"),
        name="inproj",
    )(x, w)


def _gmlp_kernel(uv_ref, w_ref, bcol_ref, g_ref, b_ref, o_ref):
    rows = uv_ref.shape[0]
    r = lax.broadcasted_iota(jnp.int32, (GM_CHUNK, GM_CHUNK), 0)
    c = lax.broadcasted_iota(jnp.int32, (GM_CHUNK, GM_CHUNK), 1)
    causal = r >= c
    for h in range(GM_HEADS):
        cols = slice(h * GM_HEAD_DIM, (h + 1) * GM_HEAD_DIM)
        vcols = slice(GM_WIDTH + h * GM_HEAD_DIM, GM_WIDTH + (h + 1) * GM_HEAD_DIM)
        wm = jnp.where(causal, w_ref[h], 0.0).astype(BF16)
        v = _gelu(uv_ref[:, vcols].astype(F32))
        vn = _layer_norm(v, g_ref[:, cols], b_ref[:, cols]).astype(BF16)
        u = _gelu(uv_ref[:, cols].astype(F32))
        for ci in range(rows // GM_CHUNK):
            rs = slice(ci * GM_CHUNK, (ci + 1) * GM_CHUNK)
            s = jnp.dot(wm, vn[rs], preferred_element_type=F32) + bcol_ref[h]
            o_ref[rs, cols] = (u[rs] * s).astype(BF16)


def _gmlp(proj, w_s, b_col, ln_g, ln_b):
    t = proj.shape[0]
    return pl.pallas_call(
        _gmlp_kernel,
        grid=(t // ROW_TILE,),
        in_specs=[pl.BlockSpec((ROW_TILE, 2 * GM_WIDTH), lambda i: (i, 0)),
                  pl.BlockSpec((GM_HEADS, GM_CHUNK, GM_CHUNK), lambda i: (0, 0, 0)),
                  pl.BlockSpec((GM_HEADS, GM_CHUNK, 1), lambda i: (0, 0, 0)),
                  pl.BlockSpec((1, GM_WIDTH), lambda i: (0, 0)),
                  pl.BlockSpec((1, GM_WIDTH), lambda i: (0, 0))],
        out_specs=pl.BlockSpec((ROW_TILE, GM_WIDTH), lambda i: (i, 0)),
        out_shape=jax.ShapeDtypeStruct((t, GM_WIDTH), BF16),
        compiler_params=_params("parallel"),
        name="gmlp",
    )(proj, w_s, b_col, ln_g, ln_b)


def _gla_kernel(q_ref, k_ref, v_ref, g_ref, z_ref, ba_ref, ng_ref, o_ref,
                st_ref, qd_ref, ki_ref, kd_ref, dec_ref, raw_ref):
    rows = q_ref.shape[0]

    @pl.when(pl.program_id(1) == 0)
    def _():
        st_ref[...] = jnp.zeros(st_ref.shape, F32)

    n_chunks = rows // GLA_CHUNK
    shift = GLA_CHUNK.bit_length() - 1
    nt = (((1,), (1,)), ((), ()))
    tn = (((0,), (0,)), ((), ()))

    r = lax.broadcasted_iota(jnp.int32, (rows, rows), 0)
    c = lax.broadcasted_iota(jnp.int32, (rows, rows), 1)
    ones_tril = jnp.where(jnp.logical_and((r >> shift) == (c >> shift), r >= c), 1.0, 0.0).astype(BF16)
    z = z_ref[...] + ba_ref[...]
    la = (jnp.minimum(z, 0.0) - jnp.log1p(jnp.exp(-jnp.abs(z)))) * (1.0 / GLA_TAU)
    la_hi = la.astype(BF16)
    la_lo = (la - la_hi.astype(F32)).astype(BF16)
    b = (jnp.dot(ones_tril, la_hi, preferred_element_type=F32)
         + jnp.dot(ones_tril, la_lo, preferred_element_type=F32))
    b3 = b.reshape(n_chunks, GLA_CHUNK, GLA_KEY_WIDTH)
    b_last = b3[:, GLA_CHUNK - 1:GLA_CHUNK, :]
    k = k_ref[...].astype(F32)
    qd_ref[...] = (q_ref[...].astype(F32) * (GLA_DK ** -0.5) * jnp.exp(b)).astype(BF16)
    ki_ref[...] = (k * jnp.exp(-b)).astype(BF16)
    kd_ref[...] = (k * jnp.exp(b_last - b3).reshape(rows, GLA_KEY_WIDTH)).astype(BF16)
    dec_ref[...] = jnp.exp(b_last).reshape(n_chunks, GLA_KEY_WIDTH)

    rc = lax.broadcasted_iota(jnp.int32, (GLA_CHUNK, GLA_CHUNK), 0)
    cc = lax.broadcasted_iota(jnp.int32, (GLA_CHUNK, GLA_CHUNK), 1)
    causal = rc >= cc

    def chunk(ci, carry):
        rs = pl.ds(pl.multiple_of(ci * GLA_CHUNK, GLA_CHUNK), GLA_CHUNK)
        q_dec = qd_ref[rs, :]
        k_inv = ki_ref[rs, :]
        k_dec = kd_ref[rs, :]
        decay = dec_ref[pl.ds(ci, 1), :]
        for h in range(GLA_HEADS):
            ks = slice(h * GLA_DK, (h + 1) * GLA_DK)
            vs = slice(h * GLA_DV, (h + 1) * GLA_DV)
            scores = lax.dot_general(q_dec[:, ks], k_inv[:, ks], nt, preferred_element_type=F32)
            scores = jnp.where(causal, scores, 0.0).astype(BF16)
            vh = v_ref[rs, vs]
            state_t = st_ref[h]
            o = (jnp.dot(scores, vh, preferred_element_type=F32)
                 + lax.dot_general(q_dec[:, ks], state_t.astype(BF16), nt, preferred_element_type=F32))
            st_ref[h] = (state_t * decay[:, ks]
                         + lax.dot_general(vh, k_dec[:, ks], tn, preferred_element_type=F32))
            raw_ref[rs, vs] = o
        return carry

    lax.fori_loop(0, rows // GLA_CHUNK, chunk, 0, unroll=GLA_UNROLL)

    for h in range(GLA_HEADS):
        vs = slice(h * GLA_DV, (h + 1) * GLA_DV)
        o = raw_ref[:, vs]
        o = o * lax.rsqrt(jnp.mean(o * o, axis=-1, keepdims=True) + RMS_EPS) * ng_ref[:, vs]
        gate = g_ref[:, vs].astype(F32)
        o_ref[:, vs] = (o * (gate * jax.nn.sigmoid(gate))).astype(BF16)


def _gla(proj, z, ba, norm_g, batch, seq):
    t = proj.shape[0]
    nseq = seq // ROW_TILE
    kw, vw = GLA_KEY_WIDTH, GLA_WIDTH
    q_blk = (2 * GM_WIDTH) // kw
    v_blk = (2 * GM_WIDTH + 2 * kw) // vw
    row = lambda b, s: b * nseq + s
    return pl.pallas_call(
        _gla_kernel,
        grid=(batch, nseq),
        in_specs=[pl.BlockSpec((ROW_TILE, kw), lambda b, s: (row(b, s), q_blk)),
                  pl.BlockSpec((ROW_TILE, kw), lambda b, s: (row(b, s), q_blk + 1)),
                  pl.BlockSpec((ROW_TILE, vw), lambda b, s: (row(b, s), v_blk)),
                  pl.BlockSpec((ROW_TILE, vw), lambda b, s: (row(b, s), v_blk + 1)),
                  pl.BlockSpec((ROW_TILE, kw), lambda b, s: (row(b, s), 0)),
                  pl.BlockSpec((1, kw), lambda b, s: (0, 0)),
                  pl.BlockSpec((1, vw), lambda b, s: (0, 0))],
        out_specs=pl.BlockSpec((ROW_TILE, vw), lambda b, s: (row(b, s), 0)),
        out_shape=jax.ShapeDtypeStruct((t, vw), BF16),
        scratch_shapes=[pltpu.VMEM((GLA_HEADS, GLA_DV, GLA_DK), F32),
                        pltpu.VMEM((ROW_TILE, kw), BF16),
                        pltpu.VMEM((ROW_TILE, kw), BF16),
                        pltpu.VMEM((ROW_TILE, kw), BF16),
                        pltpu.VMEM((ROW_TILE // GLA_CHUNK, kw), F32),
                        pltpu.VMEM((ROW_TILE, vw), F32)],
        compiler_params=_params("parallel", "arbitrary"),
        name="gla",
    )(proj, proj, proj, proj, z, ba, norm_g)


def _outproj_ln_kernel(a_ref, b_ref, w_ref, x_ref, g_ref, be_ref, o_ref):
    h = (jnp.dot(a_ref[...], w_ref[:GM_WIDTH, :], preferred_element_type=F32)
         + jnp.dot(b_ref[...], w_ref[GM_WIDTH:, :], preferred_element_type=F32))
    o_ref[...] = _layer_norm(DEEPNORM_ALPHA * x_ref[...] + h, g_ref[...], be_ref[...])


def _outproj_ln(out_a, out_b, w, x, g, b):
    t, d = x.shape
    return pl.pallas_call(
        _outproj_ln_kernel,
        grid=(t // ROW_TILE,),
        in_specs=[pl.BlockSpec((ROW_TILE, GM_WIDTH), lambda i: (i, 0)),
                  pl.BlockSpec((ROW_TILE, GLA_WIDTH), lambda i: (i, 0)),
                  pl.BlockSpec((GM_WIDTH + GLA_WIDTH, d), lambda i: (0, 0)),
                  pl.BlockSpec((ROW_TILE, d), lambda i: (i, 0)),
                  pl.BlockSpec((1, d), lambda i: (0, 0)),
                  pl.BlockSpec((1, d), lambda i: (0, 0))],
        out_specs=pl.BlockSpec((ROW_TILE, d), lambda i: (i, 0)),
        out_shape=jax.ShapeDtypeStruct((t, d), F32),
        compiler_params=_params("parallel"),
        name="outproj_ln",
    )(out_a, out_b, w, x, g, b)


def _swiglu_step(xb, wg_ref, wu_ref, wd_ref):
    gate = jnp.dot(xb, wg_ref[...], preferred_element_type=F32)
    up = jnp.dot(xb, wu_ref[...], preferred_element_type=F32)
    hid = (gate * jax.nn.sigmoid(gate) * up).astype(BF16)
    return jnp.dot(hid, wd_ref[...], preferred_element_type=F32)


def _swiglu_ln_kernel(x_ref, wg_ref, wu_ref, wd_ref, g_ref, b_ref, o_ref):
    xb = x_ref[...].astype(BF16)
    acc = jnp.zeros(x_ref.shape, F32)
    for n in range(wg_ref.shape[1] // DENSE_FF_TILE):
        cols = slice(n * DENSE_FF_TILE, (n + 1) * DENSE_FF_TILE)
        gate = jnp.dot(xb, wg_ref[:, cols], preferred_element_type=F32)
        up = jnp.dot(xb, wu_ref[:, cols], preferred_element_type=F32)
        hid = (gate * jax.nn.sigmoid(gate) * up).astype(BF16)
        acc = acc + jnp.dot(hid, wd_ref[cols, :], preferred_element_type=F32)
    o_ref[...] = _layer_norm(DEEPNORM_ALPHA * x_ref[...] + acc, g_ref[...], b_ref[...])


def _swiglu_ln(x, wg, wu, wd, g, b):
    rows, d = x.shape
    ff = wg.shape[1]
    resident = pl.Buffered(1)
    return pl.pallas_call(
        _swiglu_ln_kernel,
        grid=(rows // ROW_TILE,),
        in_specs=[pl.BlockSpec((ROW_TILE, d), lambda i: (i, 0)),
                  pl.BlockSpec((d, ff), lambda i: (0, 0), pipeline_mode=resident),
                  pl.BlockSpec((d, ff), lambda i: (0, 0), pipeline_mode=resident),
                  pl.BlockSpec((ff, d), lambda i: (0, 0), pipeline_mode=resident),
                  pl.BlockSpec((1, d), lambda i: (0, 0)),
                  pl.BlockSpec((1, d), lambda i: (0, 0))],
        out_specs=pl.BlockSpec((ROW_TILE, d), lambda i: (i, 0)),
        out_shape=jax.ShapeDtypeStruct((rows, d), F32),
        compiler_params=_params("parallel"),
        name="swiglu_ln",
    )(x, wg, wu, wd, g, b)


def _router_kernel(x_ref, wr_ref, ri_ref, rw_ref, cnt_ref):
    tb = x_ref.shape[0]
    logits = lax.dot_general(wr_ref[...], x_ref[...], (((1,), (1,)), ((), ())),
                             precision=lax.Precision.HIGHEST, preferred_element_type=F32)
    e_iota = lax.broadcasted_iota(jnp.int32, (N_EXPERTS, tb), 0)
    m0 = jnp.max(logits, axis=0, keepdims=True)
    i0 = jnp.min(jnp.where(logits == m0, e_iota, N_EXPERTS), axis=0, keepdims=True)
    rest = jnp.where(e_iota == i0, -jnp.inf, logits)
    m1 = jnp.max(rest, axis=0, keepdims=True)
    i1 = jnp.min(jnp.where(rest == m1, e_iota, N_EXPERTS), axis=0, keepdims=True)
    ex = jnp.exp(m1 - m0)
    w0 = 1.0 / (1.0 + ex)
    w1 = ex / (1.0 + ex)
    oh0 = e_iota == i0
    oh1 = e_iota == i1
    onehot = jnp.where(jnp.logical_or(oh0, oh1), 1.0, 0.0)
    tr = lax.broadcasted_iota(jnp.int32, (tb, tb), 0)
    tc = lax.broadcasted_iota(jnp.int32, (tb, tb), 1)
    before = jnp.where(tr < tc, 1.0, 0.0).astype(BF16)
    rank = jnp.dot(onehot.astype(BF16), before, preferred_element_type=F32)
    count = jnp.broadcast_to(jnp.sum(onehot, axis=1, keepdims=True), (N_EXPERTS, LANE))
    cap = jnp.ceil(count * (1.0 / GRANULE)) * GRANULE
    e_sub = lax.broadcasted_iota(jnp.int32, (N_EXPERTS, LANE), 0)
    seg = jnp.zeros((N_EXPERTS, LANE), F32)
    for e in range(N_EXPERTS - 1):
        seg = seg + jnp.where(e_sub > e, cap[e:e + 1, :], 0.0)
    row = rank + seg[:, 0:1]
    row0 = jnp.sum(jnp.where(oh0, row, 0.0), axis=0, keepdims=True)
    row1 = jnp.sum(jnp.where(oh1, row, 0.0), axis=0, keepdims=True)
    cnt_ref[...] = count
    ri_ref[...] = jnp.zeros(ri_ref.shape, jnp.int32)
    ri_ref[0:1, :] = i0
    ri_ref[1:2, :] = i1
    ri_ref[2:3, :] = row0.astype(jnp.int32)
    ri_ref[3:4, :] = row1.astype(jnp.int32)
    rw_ref[...] = jnp.zeros(rw_ref.shape, F32)
    rw_ref[0:1, :] = w0
    rw_ref[1:2, :] = w1


def _route_dispatch_kernel(x_ref, wr_ref, ri_ref, rw_ref, cnt_ref, o_ref):
    _router_kernel(x_ref, wr_ref, ri_ref, rw_ref, cnt_ref)
    rows, tb = o_ref.shape[0], x_ref.shape[0]
    r = lax.broadcasted_iota(jnp.int32, (rows, tb), 0)
    sel = jnp.logical_or(r == ri_ref[2:3, :], r == ri_ref[3:4, :])
    perm = jnp.where(sel, 1.0, 0.0).astype(BF16)
    o_ref[...] = jnp.dot(perm, x_ref[...].astype(BF16), preferred_element_type=F32).astype(BF16)


def _route_dispatch(x, wr_t):
    t, d = x.shape
    nb = t // MOE_BLOCK
    return pl.pallas_call(
        _route_dispatch_kernel,
        grid=(nb,),
        in_specs=[pl.BlockSpec((MOE_BLOCK, d), lambda i: (i, 0)),
                  pl.BlockSpec((N_EXPERTS, d), lambda i: (0, 0))],
        out_specs=[pl.BlockSpec((SUBLANE, MOE_BLOCK), lambda i: (0, i)),
                   pl.BlockSpec((SUBLANE, MOE_BLOCK), lambda i: (0, i)),
                   pl.BlockSpec((None, N_EXPERTS, LANE), lambda i: (i, 0, 0)),
                   pl.BlockSpec((LOC_ROWS, d), lambda i: (i, 0))],
        out_shape=[jax.ShapeDtypeStruct((SUBLANE, t), jnp.int32),
                   jax.ShapeDtypeStruct((SUBLANE, t), F32),
                   jax.ShapeDtypeStruct((nb, N_EXPERTS, LANE), F32),
                   jax.ShapeDtypeStruct((nb * LOC_ROWS, d), BF16)],
        compiler_params=_params("parallel"),
        name="route_dispatch",
    )(x, wr_t)


def _experts_kernel(te_ref, nu_ref, gran_ref, x_hbm, wg_hbm, wu_hbm, wd_hbm, yinit_hbm, y_hbm,
                    xbuf, ybuf, acc_ref, wg_buf, wu_buf, wd_buf, sem_in, sem_out, sem_w):
    del yinit_hbm
    i = pl.program_id(0)
    n_tiles = pl.num_programs(0)
    n_used = nu_ref[0]
    n_chunks = wg_buf.shape[0]
    expert = te_ref[i]
    prev_expert = te_ref[jnp.maximum(i - 1, 0)]
    next_expert = te_ref[jnp.minimum(i + 1, n_tiles - 1)]
    first_of_expert = jnp.logical_or(i == 0, prev_expert != expert)
    refill = jnp.logical_and(i + 1 < n_used, next_expert != expert)

    def weight_copies(e, n):
        cols = pl.ds(n * MOE_FF_TILE, MOE_FF_TILE)
        return (pltpu.make_async_copy(wg_hbm.at[e, :, cols], wg_buf.at[n], sem_w.at[n]),
                pltpu.make_async_copy(wu_hbm.at[e, :, cols], wu_buf.at[n], sem_w.at[n]),
                pltpu.make_async_copy(wd_hbm.at[e, cols, :], wd_buf.at[n], sem_w.at[n]))

    def for_granules(tile, fn):
        def body(s, carry):
            g = gran_ref[tile * GRAN_PER_TILE + s]

            @pl.when(g >= 0)
            def _():
                fn(pl.ds(pl.multiple_of(g * GRANULE, GRANULE), GRANULE),
                   pl.ds(pl.multiple_of(s * GRANULE, GRANULE), GRANULE))
            return carry
        lax.fori_loop(0, GRAN_PER_TILE, body, 0)

    def gather(tile, slot):
        return lambda src, dst: pltpu.make_async_copy(x_hbm.at[src], xbuf.at[slot, dst],
                                                      sem_in.at[slot])

    def scatter(src, dst):
        return pltpu.make_async_copy(ybuf.at[dst], y_hbm.at[src], sem_out.at[0])

    @pl.when(i < n_used)
    def _():
        slot = i % 2

        @pl.when(i == 0)
        def _():
            for n in range(n_chunks):
                for cp in weight_copies(expert, n):
                    cp.start()
            xbuf[...] = jnp.zeros(xbuf.shape, BF16)
            for_granules(0, lambda s, d: gather(0, 0)(s, d).start())

        for_granules(i, lambda s, d: gather(i, slot)(s, d).wait())

        @pl.when(i + 1 < n_used)
        def _():
            for_granules(i + 1, lambda s, d: gather(i + 1, 1 - slot)(s, d).start())

        xb = xbuf[slot]
        for n in range(n_chunks):
            @pl.when(first_of_expert)
            def _():
                for cp in weight_copies(expert, n):
                    cp.wait()

            part = _swiglu_step(xb, wg_buf.at[n], wu_buf.at[n], wd_buf.at[n])
            if n == 0:
                acc_ref[...] = part
            else:
                acc_ref[...] += part

            @pl.when(refill)
            def _():
                for cp in weight_copies(next_expert, n):
                    cp.start()

        @pl.when(i > 0)
        def _():
            for_granules(i - 1, lambda s, d: scatter(s, d).wait())

        ybuf[...] = acc_ref[...].astype(BF16)
        for_granules(i, lambda s, d: scatter(s, d).start())

        @pl.when(i == n_used - 1)
        def _():
            for_granules(i, lambda s, d: scatter(s, d).wait())


def _experts(tile_expert, n_used, gran, x_loc, wg, wu, wd):
    rows, d = x_loc.shape
    ff = wg.shape[2]
    n_tiles = tile_expert.shape[0]
    nj = ff // MOE_FF_TILE
    hbm = pl.BlockSpec(memory_space=pl.ANY)
    return pl.pallas_call(
        _experts_kernel,
        grid_spec=pltpu.PrefetchScalarGridSpec(
            num_scalar_prefetch=3,
            grid=(n_tiles,),
            in_specs=[hbm, hbm, hbm, hbm, hbm],
            out_specs=hbm,
            scratch_shapes=[pltpu.VMEM((2, FFN_ROW_TILE, d), BF16),
                            pltpu.VMEM((FFN_ROW_TILE, d), BF16),
                            pltpu.VMEM((FFN_ROW_TILE, d), F32),
                            pltpu.VMEM((nj, d, MOE_FF_TILE), BF16),
                            pltpu.VMEM((nj, d, MOE_FF_TILE), BF16),
                            pltpu.VMEM((nj, MOE_FF_TILE, d), BF16),
                            pltpu.SemaphoreType.DMA((2,)),
                            pltpu.SemaphoreType.DMA((1,)),
                            pltpu.SemaphoreType.DMA((nj,))]),
        out_shape=jax.ShapeDtypeStruct((rows, d), BF16),
        input_output_aliases={7: 0},
        compiler_params=_params("arbitrary"),
        name="swiglu_experts",
    )(tile_expert, n_used, gran, x_loc, wg, wu, wd, jnp.zeros((rows, d), BF16))


def _combine_ln_kernel(y_ref, row_ref, w_ref, x_ref, g_ref, b_ref, o_ref):
    tb, rows = x_ref.shape[0], y_ref.shape[0]
    c = lax.broadcasted_iota(jnp.int32, (tb, rows), 1)
    q = (jnp.where(c == row_ref[:, 0:1], w_ref[:, 0:1], 0.0)
         + jnp.where(c == row_ref[:, 1:2], w_ref[:, 1:2], 0.0))
    f = jnp.dot(q.astype(BF16), y_ref[...], preferred_element_type=F32)
    o_ref[...] = _layer_norm(DEEPNORM_ALPHA * x_ref[...] + f, g_ref[...], b_ref[...])


def _combine_ln(y_loc, row_col, w_col, x, g, b):
    t, d = x.shape
    return pl.pallas_call(
        _combine_ln_kernel,
        grid=(t // MOE_BLOCK,),
        in_specs=[pl.BlockSpec((LOC_ROWS, d), lambda i: (i, 0)),
                  pl.BlockSpec((MOE_BLOCK, TOP_K), lambda i: (i, 0)),
                  pl.BlockSpec((MOE_BLOCK, TOP_K), lambda i: (i, 0)),
                  pl.BlockSpec((MOE_BLOCK, d), lambda i: (i, 0)),
                  pl.BlockSpec((1, d), lambda i: (0, 0)),
                  pl.BlockSpec((1, d), lambda i: (0, 0))],
        out_specs=pl.BlockSpec((MOE_BLOCK, d), lambda i: (i, 0)),
        out_shape=jax.ShapeDtypeStruct((t, d), F32),
        compiler_params=_params("parallel"),
        name="combine_ln",
    )(y_loc, row_col, w_col, x, g, b)


def _mixer(x, batch, seq, w_in, gm_ws, gm_bs, gm_ln_g, gm_ln_b, gla_wa2, gla_ba, gla_norm_g,
           w_out, ln_g, ln_b):
    w_gate = _fold_gate(w_in[:, PROJ_MAIN:], gla_wa2)
    w_ext = jnp.concatenate([w_in[:, :PROJ_MAIN], w_gate], axis=1).astype(BF16)
    proj, z = _inproj(x, w_ext)
    out_a = _gmlp(proj, gm_ws, gm_bs.reshape(GM_HEADS, GM_CHUNK, 1),
                  gm_ln_g.reshape(1, -1), gm_ln_b.reshape(1, -1))
    out_b = _gla(proj, z, gla_ba.reshape(1, -1), gla_norm_g.reshape(1, -1), batch, seq)
    return _outproj_ln(out_a, out_b, w_out.astype(BF16), x, ln_g.reshape(1, -1), ln_b.reshape(1, -1))


def _granule_table(counts):
    nb = counts.shape[0]
    n_tiles = (nb * (TOP_K * MOE_BLOCK + N_EXPERTS * (GRANULE - 1))) // FFN_ROW_TILE + N_EXPERTS
    gran_per_block = LOC_ROWS // GRANULE
    g = (counts + GRANULE - 1) // GRANULE
    seg_start = jnp.cumsum(g, axis=1) - g
    cum_incl = jnp.cumsum(g, axis=0)
    cum_excl = cum_incl - g
    total = cum_incl[-1]
    tiles_per_expert = (total + GRAN_PER_TILE - 1) // GRAN_PER_TILE
    tile_end = jnp.cumsum(tiles_per_expert)
    n_used = tile_end[-1:]
    tile = jnp.arange(n_tiles, dtype=jnp.int32)
    tile_expert = jnp.minimum(jnp.sum(tile[:, None] >= tile_end[None, :], axis=1), N_EXPERTS - 1)
    pick = (tile_expert[:, None] == jnp.arange(N_EXPERTS, dtype=jnp.int32)[None, :]).astype(jnp.int32)
    per_tile = lambda tab: jnp.sum(pick[:, :, None] * tab.T[None, :, :], axis=1)
    first_tile = jnp.sum(pick * (tile_end - tiles_per_expert)[None, :], axis=1)
    vg = ((tile - first_tile)[:, None] * GRAN_PER_TILE
          + jnp.arange(GRAN_PER_TILE, dtype=jnp.int32)[None, :])[:, :, None]
    lo = per_tile(cum_excl)[:, None, :]
    hi = per_tile(cum_incl)[:, None, :]
    base = (jnp.arange(nb, dtype=jnp.int32)[None, :] * gran_per_block + per_tile(seg_start))[:, None, :]
    inside = jnp.logical_and(vg >= lo, vg < hi)
    phys = jnp.sum(jnp.where(inside, base + vg - lo, 0), axis=2)
    valid = jnp.logical_and(jnp.any(inside, axis=2), (tile < n_used[0])[:, None])
    gran = jnp.where(valid, phys, -1).astype(jnp.int32).reshape(-1)
    return tile_expert.astype(jnp.int32), n_used.astype(jnp.int32), gran


def _moe_ffn(x, w_router, e_gate, e_up, e_down, ln_g, ln_b):
    ri, rw, cnt, x_loc = _route_dispatch(x, w_router.T)
    tile_expert, n_used, gran = _granule_table(cnt[:, :, 0].astype(jnp.int32))
    y_loc = _experts(tile_expert, n_used, gran, x_loc,
                     e_gate.astype(BF16), e_up.astype(BF16), e_down.astype(BF16))
    return _combine_ln(y_loc, ri[2:2 + TOP_K].T, rw[:TOP_K].T, x,
                       ln_g.reshape(1, -1), ln_b.reshape(1, -1))


def kernel(x, w_in, gm_ws, gm_bs, gm_ln_g, gm_ln_b, gla_wa2, gla_ba, gla_norm_g, w_out, ln_mix_g, ln_mix_b, ffn_w_gate, ffn_w_up, ffn_w_down, router_w, exp_w_gate, exp_w_up, exp_w_down, ln_ffn_g, ln_ffn_b):
    batch, seq, d = x.shape
    h = x.reshape(batch * seq, d)
    for layer in range(DEPTH):
        h = _mixer(h, batch, seq, w_in[layer], gm_ws[layer], gm_bs[layer], gm_ln_g[layer],
                   gm_ln_b[layer], gla_wa2[layer], gla_ba[layer], gla_norm_g[layer], w_out[layer],
                   ln_mix_g[layer], ln_mix_b[layer])
        i = layer // 2
        if layer % 2 == 0:
            h = _swiglu_ln(h, ffn_w_gate[i].astype(BF16), ffn_w_up[i].astype(BF16),
                           ffn_w_down[i].astype(BF16), ln_ffn_g[layer].reshape(1, -1),
                           ln_ffn_b[layer].reshape(1, -1))
        else:
            h = _moe_ffn(h, router_w[i], exp_w_gate[i], exp_w_up[i], exp_w_down[i],
                         ln_ffn_g[layer], ln_ffn_b[layer])
    return h.reshape(batch, seq, d)
```

```python
import functools

import jax
import jax.numpy as jnp
from jax import lax
from jax.experimental import pallas as pl
from jax.experimental.pallas import tpu as pltpu

F32 = jnp.float32
BF16 = jnp.bfloat16

D_MODEL = 1024
DEPTH = 2
GM_HEADS = 4
GM_WIDTH = 512
GM_HEAD_DIM = 128
GM_CHUNK = 128
GLA_HEADS = 4
GLA_WIDTH = 512
GLA_DV = 128
GLA_KEY_WIDTH = 256
GLA_DK = 64
GLA_GATE_RANK = 16
GLA_TAU = 16.0
GLA_CHUNK = 64
PROJ_MAIN = 2 * GM_WIDTH + 2 * GLA_KEY_WIDTH + 2 * GLA_WIDTH
N_EXPERTS = 8
TOP_K = 2
LN_EPS = 1e-5
RMS_EPS = 1e-6
DEEPNORM_ALPHA = (2 * DEPTH) ** 0.25

V7X_VMEM_LIMIT_BYTES = 56 * 1024 * 1024
LANE = 128
SUBLANE = 8
MXU_N = 256

ROW_TILE = 512
FFN_ROW_TILE = 1024
DENSE_FF_TILE = 256
MOE_FF_TILE = 512
MOE_BLOCK = 512
GRANULE = 16
LOC_ROWS = -(-(TOP_K * MOE_BLOCK + N_EXPERTS * (GRANULE - 1)) // LANE) * LANE
GRAN_PER_TILE = FFN_ROW_TILE // GRANULE


def _params(*sem, **kw):
    return pltpu.CompilerParams(dimension_semantics=sem, vmem_limit_bytes=V7X_VMEM_LIMIT_BYTES, **kw)


def _gelu(x):
    return 0.5 * x * (1.0 + lax.erf(x * (2.0 ** -0.5)))


def _layer_norm(y, g, b):
    mu = jnp.mean(y, axis=-1, keepdims=True)
    d = y - mu
    var = jnp.mean(d * d, axis=-1, keepdims=True)
    return d * lax.rsqrt(var + LN_EPS) * g + b


def _fold_kernel(w1_ref, w2_ref, o_ref):
    acc = jnp.zeros(o_ref.shape, F32)
    for r in range(GLA_GATE_RANK):
        acc = acc + w1_ref[:, r:r + 1] * w2_ref[r:r + 1, :]
    o_ref[...] = acc


def _fold_gate(w1, w2):
    return pl.pallas_call(
        _fold_kernel,
        out_shape=jax.ShapeDtypeStruct((w1.shape[0], w2.shape[1]), F32),
        name="fold_gate",
    )(w1, w2)


def _inproj_items(x_ref, w_ref, p_ref, z_ref):
    cache = {}

    def xb():
        if "xb" not in cache:
            cache["xb"] = x_ref[...].astype(BF16)
        return cache["xb"]

    def block(n):
        cols = slice(n * MXU_N, (n + 1) * MXU_N)

        def run():
            p_ref[:, cols] = jnp.dot(xb(), w_ref[:, cols], preferred_element_type=F32).astype(BF16)
        return run

    def gate_logits():
        z_ref[...] = jnp.dot(xb(), w_ref[:, PROJ_MAIN:], preferred_element_type=F32)

    return [block(n) for n in range(PROJ_MAIN // MXU_N)] + [gate_logits]


def _gmlp_items(uv_ref, w_ref, bcol_ref, g_ref, b_ref, o_ref):
    rows = uv_ref.shape[0]

    def head(h):
        cols = slice(h * GM_HEAD_DIM, (h + 1) * GM_HEAD_DIM)
        vcols = slice(GM_WIDTH + h * GM_HEAD_DIM, GM_WIDTH + (h + 1) * GM_HEAD_DIM)

        def run():
            r = lax.broadcasted_iota(jnp.int32, (GM_CHUNK, GM_CHUNK), 0)
            c = lax.broadcasted_iota(jnp.int32, (GM_CHUNK, GM_CHUNK), 1)
            wm = jnp.where(r >= c, w_ref[h], 0.0).astype(BF16)
            v = _gelu(uv_ref[:, vcols].astype(F32))
            vn = _layer_norm(v, g_ref[:, cols], b_ref[:, cols]).astype(BF16)
            u = _gelu(uv_ref[:, cols].astype(F32))
            for ci in range(rows // GM_CHUNK):
                rs = slice(ci * GM_CHUNK, (ci + 1) * GM_CHUNK)
                s = jnp.dot(wm, vn[rs], preferred_element_type=F32) + bcol_ref[h]
                o_ref[rs, cols] = (u[rs] * s).astype(BF16)
        return run

    return [head(h) for h in range(GM_HEADS)]


def _gla_items(q_ref, k_ref, v_ref, g_ref, z_ref, ba_ref, ng_ref, o_ref,
               st_ref, qd_ref, ki_ref, kd_ref, dec_ref, raw_ref):
    rows = q_ref.shape[0]
    n_chunks = rows // GLA_CHUNK
    shift = GLA_CHUNK.bit_length() - 1
    nt = (((1,), (1,)), ((), ()))
    tn = (((0,), (0,)), ((), ()))

    def prep():
        r = lax.broadcasted_iota(jnp.int32, (rows, rows), 0)
        c = lax.broadcasted_iota(jnp.int32, (rows, rows), 1)
        ones_tril = jnp.where(jnp.logical_and((r >> shift) == (c >> shift), r >= c),
                              1.0, 0.0).astype(BF16)
        z = z_ref[...] + ba_ref[...]
        la = (jnp.minimum(z, 0.0) - jnp.log1p(jnp.exp(-jnp.abs(z)))) * (1.0 / GLA_TAU)
        la_hi = la.astype(BF16)
        la_lo = (la - la_hi.astype(F32)).astype(BF16)
        b = (jnp.dot(ones_tril, la_hi, preferred_element_type=F32)
             + jnp.dot(ones_tril, la_lo, preferred_element_type=F32))
        b3 = b.reshape(n_chunks, GLA_CHUNK, GLA_KEY_WIDTH)
        b_last = b3[:, GLA_CHUNK - 1:GLA_CHUNK, :]
        k = k_ref[...].astype(F32)
        qd_ref[...] = (q_ref[...].astype(F32) * (GLA_DK ** -0.5) * jnp.exp(b)).astype(BF16)
        ki_ref[...] = (k * jnp.exp(-b)).astype(BF16)
        kd_ref[...] = (k * jnp.exp(b_last - b3).reshape(rows, GLA_KEY_WIDTH)).astype(BF16)
        dec_ref[...] = jnp.exp(b_last).reshape(n_chunks, GLA_KEY_WIDTH)

    def chunk(ci):
        rs = slice(ci * GLA_CHUNK, (ci + 1) * GLA_CHUNK)

        def run():
            rc = lax.broadcasted_iota(jnp.int32, (GLA_CHUNK, GLA_CHUNK), 0)
            cc = lax.broadcasted_iota(jnp.int32, (GLA_CHUNK, GLA_CHUNK), 1)
            causal = rc >= cc
            q_dec = qd_ref[rs, :]
            k_inv = ki_ref[rs, :]
            k_dec = kd_ref[rs, :]
            decay = dec_ref[ci:ci + 1, :]
            for h in range(GLA_HEADS):
                ks = slice(h * GLA_DK, (h + 1) * GLA_DK)
                vs = slice(h * GLA_DV, (h + 1) * GLA_DV)
                scores = lax.dot_general(q_dec[:, ks], k_inv[:, ks], nt, preferred_element_type=F32)
                scores = jnp.where(causal, scores, 0.0).astype(BF16)
                vh = v_ref[rs, vs]
                state_t = st_ref[h]
                o = (jnp.dot(scores, vh, preferred_element_type=F32)
                     + lax.dot_general(q_dec[:, ks], state_t.astype(BF16), nt,
                                       preferred_element_type=F32))
                st_ref[h] = (state_t * decay[:, ks]
                             + lax.dot_general(vh, k_dec[:, ks], tn, preferred_element_type=F32))
                raw_ref[rs, vs] = o
        return run

    def finish():
        for h in range(GLA_HEADS):
            vs = slice(h * GLA_DV, (h + 1) * GLA_DV)
            o = raw_ref[:, vs]
            o = o * lax.rsqrt(jnp.mean(o * o, axis=-1, keepdims=True) + RMS_EPS) * ng_ref[:, vs]
            gate = g_ref[:, vs].astype(F32)
            o_ref[:, vs] = (o * (gate * jax.nn.sigmoid(gate))).astype(BF16)

    return prep, [chunk(ci) for ci in range(n_chunks)], finish


def _mixer_kernel(xa_ref, xc_ref, win_ref, wout_ref, lng_ref, lnb_ref,
                  gw_ref, gbcol_ref, gg_ref, gb_ref, ba_ref, ng_ref, o_ref,
                  proj0, proj1, z0, z1, ab0, ab1, st_ref, *gla_scratch, tiles_per_seq):
    t = pl.program_id(0)
    proj_scr, z_scr, ab_scr = (proj0, proj1), (z0, z1), (ab0, ab1)

    @pl.when(t == 0)
    def _():
        for ref in (proj0, proj1, z0, z1, ab0, ab1, st_ref):
            ref[...] = jnp.zeros(ref.shape, ref.dtype)

    @pl.when(lax.rem(t + tiles_per_seq - 1, tiles_per_seq) == 0)
    def _():
        st_ref[...] = jnp.zeros(st_ref.shape, F32)

    q0 = 2 * GM_WIDTH
    k0 = q0 + GLA_KEY_WIDTH
    v0 = k0 + GLA_KEY_WIDTH
    g0 = v0 + GLA_WIDTH

    def stages(proj_a, z_a, proj_b, z_b, ab_b, ab_c):
        inproj = _inproj_items(xa_ref, win_ref, proj_a, z_a)
        gla_prep, gla_chunks, gla_finish = _gla_items(
            proj_b.at[:, q0:k0], proj_b.at[:, k0:v0], proj_b.at[:, v0:g0],
            proj_b.at[:, g0:PROJ_MAIN], z_b, ba_ref, ng_ref, ab_b.at[:, GM_WIDTH:],
            st_ref, *gla_scratch)
        gmlp = _gmlp_items(proj_b.at[:, :q0], gw_ref, gbcol_ref, gg_ref, gb_ref,
                           ab_b.at[:, :GM_WIDTH])
        outproj, outproj_finish = _outproj_items(ab_c.at[:, :GM_WIDTH], ab_c.at[:, GM_WIDTH:],
                                                 wout_ref, xc_ref, lng_ref, lnb_ref, o_ref)
        for run in inproj + [gla_prep] + gla_chunks + [gla_finish] + gmlp + outproj + [outproj_finish]:
            run()

    for parity in range(2):
        @pl.when(t % 2 == parity)
        def _(a=parity, b=1 - parity):
            stages(proj_scr[a], z_scr[a], proj_scr[b], z_scr[b], ab_scr[b], ab_scr[a])


def _mixer_call(x, w_ext, w_out, ln_g, ln_b, gm_ws, gm_bcol, gm_ln_g, gm_ln_b, ba, norm_g, seq):
    t, d = x.shape
    n_tiles = t // ROW_TILE
    kw, vw = GLA_KEY_WIDTH, GLA_WIDTH
    const2 = lambda i: (0, 0)
    const3 = lambda i: (0, 0, 0)
    resident = pl.Buffered(1)
    tile_c = lambda i: (jnp.maximum(i - 2, 0), 0)
    return pl.pallas_call(
        functools.partial(_mixer_kernel, tiles_per_seq=seq // ROW_TILE),
        grid=(n_tiles + 2,),
        in_specs=[pl.BlockSpec((ROW_TILE, d), lambda i: (jnp.minimum(i, n_tiles - 1), 0)),
                  pl.BlockSpec((ROW_TILE, d), tile_c),
                  pl.BlockSpec(w_ext.shape, const2, pipeline_mode=resident),
                  pl.BlockSpec(w_out.shape, const2, pipeline_mode=resident),
                  pl.BlockSpec((1, d), const2),
                  pl.BlockSpec((1, d), const2),
                  pl.BlockSpec((GM_HEADS, GM_CHUNK, GM_CHUNK), const3),
                  pl.BlockSpec((GM_HEADS, GM_CHUNK, 1), const3),
                  pl.BlockSpec((1, GM_WIDTH), const2),
                  pl.BlockSpec((1, GM_WIDTH), const2),
                  pl.BlockSpec((1, kw), const2),
                  pl.BlockSpec((1, vw), const2)],
        out_specs=pl.BlockSpec((ROW_TILE, d), tile_c),
        out_shape=jax.ShapeDtypeStruct((t, d), F32),
        scratch_shapes=[pltpu.VMEM((ROW_TILE, PROJ_MAIN), BF16),
                        pltpu.VMEM((ROW_TILE, PROJ_MAIN), BF16),
                        pltpu.VMEM((ROW_TILE, kw), F32),
                        pltpu.VMEM((ROW_TILE, kw), F32),
                        pltpu.VMEM((ROW_TILE, GM_WIDTH + vw), BF16),
                        pltpu.VMEM((ROW_TILE, GM_WIDTH + vw), BF16),
                        pltpu.VMEM((GLA_HEADS, GLA_DV, GLA_DK), F32),
                        pltpu.VMEM((ROW_TILE, kw), BF16),
                        pltpu.VMEM((ROW_TILE, kw), BF16),
                        pltpu.VMEM((ROW_TILE, kw), BF16),
                        pltpu.VMEM((ROW_TILE // GLA_CHUNK, kw), F32),
                        pltpu.VMEM((ROW_TILE, vw), F32)],
        compiler_params=_params("arbitrary"),
        name="mixer",
    )(x, x, w_ext, w_out, ln_g, ln_b, gm_ws, gm_bcol, gm_ln_g, gm_ln_b, ba, norm_g)


def _outproj_items(a_ref, b_ref, w_ref, x_ref, g_ref, be_ref, o_ref):
    def block(n):
        cols = slice(n * MXU_N, (n + 1) * MXU_N)

        def run():
            h = (jnp.dot(a_ref[...], w_ref[:GM_WIDTH, cols], preferred_element_type=F32)
                 + jnp.dot(b_ref[...], w_ref[GM_WIDTH:, cols], preferred_element_type=F32))
            o_ref[:, cols] = DEEPNORM_ALPHA * x_ref[:, cols] + h
        return run

    def finish():
        o_ref[...] = _layer_norm(o_ref[...], g_ref[...], be_ref[...])

    return [block(n) for n in range(o_ref.shape[1] // MXU_N)], finish


def _swiglu_step(xb, wg_ref, wu_ref, wd_ref):
    gate = jnp.dot(xb, wg_ref[...], preferred_element_type=F32)
    up = jnp.dot(xb, wu_ref[...], preferred_element_type=F32)
    hid = (gate * jax.nn.sigmoid(gate) * up).astype(BF16)
    return jnp.dot(hid, wd_ref[...], preferred_element_type=F32)


def _swiglu_ln_kernel(x_ref, wg_ref, wu_ref, wd_ref, g_ref, b_ref, o_ref):
    xb = x_ref[...].astype(BF16)
    acc = jnp.zeros(x_ref.shape, F32)
    for n in range(wg_ref.shape[1] // DENSE_FF_TILE):
        cols = slice(n * DENSE_FF_TILE, (n + 1) * DENSE_FF_TILE)
        gate = jnp.dot(xb, wg_ref[:, cols], preferred_element_type=F32)
        up = jnp.dot(xb, wu_ref[:, cols], preferred_element_type=F32)
        hid = (gate * jax.nn.sigmoid(gate) * up).astype(BF16)
        acc = acc + jnp.dot(hid, wd_ref[cols, :], preferred_element_type=F32)
    o_ref[...] = _layer_norm(DEEPNORM_ALPHA * x_ref[...] + acc, g_ref[...], b_ref[...])


def _swiglu_ln(x, wg, wu, wd, g, b):
    rows, d = x.shape
    ff = wg.shape[1]
    resident = pl.Buffered(1)
    return pl.pallas_call(
        _swiglu_ln_kernel,
        grid=(rows // ROW_TILE,),
        in_specs=[pl.BlockSpec((ROW_TILE, d), lambda i: (i, 0)),
                  pl.BlockSpec((d, ff), lambda i: (0, 0), pipeline_mode=resident),
                  pl.BlockSpec((d, ff), lambda i: (0, 0), pipeline_mode=resident),
                  pl.BlockSpec((ff, d), lambda i: (0, 0), pipeline_mode=resident),
                  pl.BlockSpec((1, d), lambda i: (0, 0)),
                  pl.BlockSpec((1, d), lambda i: (0, 0))],
        out_specs=pl.BlockSpec((ROW_TILE, d), lambda i: (i, 0)),
        out_shape=jax.ShapeDtypeStruct((rows, d), F32),
        compiler_params=_params("parallel"),
        name="swiglu_ln",
    )(x, wg, wu, wd, g, b)


def _router_kernel(x_ref, wr_ref, ri_ref, rw_ref, cnt_ref):
    tb = x_ref.shape[0]
    logits = lax.dot_general(wr_ref[...], x_ref[...], (((1,), (1,)), ((), ())),
                             precision=lax.Precision.HIGHEST, preferred_element_type=F32)
    e_iota = lax.broadcasted_iota(jnp.int32, (N_EXPERTS, tb), 0)
    m0 = jnp.max(logits, axis=0, keepdims=True)
    i0 = jnp.min(jnp.where(logits == m0, e_iota, N_EXPERTS), axis=0, keepdims=True)
    rest = jnp.where(e_iota == i0, -jnp.inf, logits)
    m1 = jnp.max(rest, axis=0, keepdims=True)
    i1 = jnp.min(jnp.where(rest == m1, e_iota, N_EXPERTS), axis=0, keepdims=True)
    ex = jnp.exp(m1 - m0)
    w0 = 1.0 / (1.0 + ex)
    w1 = ex / (1.0 + ex)
    oh0 = e_iota == i0
    oh1 = e_iota == i1
    onehot = jnp.where(jnp.logical_or(oh0, oh1), 1.0, 0.0)
    tr = lax.broadcasted_iota(jnp.int32, (tb, tb), 0)
    tc = lax.broadcasted_iota(jnp.int32, (tb, tb), 1)
    before = jnp.where(tr < tc, 1.0, 0.0).astype(BF16)
    rank = jnp.dot(onehot.astype(BF16), before, preferred_element_type=F32)
    count = jnp.broadcast_to(jnp.sum(onehot, axis=1, keepdims=True), (N_EXPERTS, LANE))
    cap = jnp.ceil(count * (1.0 / GRANULE)) * GRANULE
    e_sub = lax.broadcasted_iota(jnp.int32, (N_EXPERTS, LANE), 0)
    seg = jnp.zeros((N_EXPERTS, LANE), F32)
    for e in range(N_EXPERTS - 1):
        seg = seg + jnp.where(e_sub > e, cap[e:e + 1, :], 0.0)
    row = rank + seg[:, 0:1]
    row0 = jnp.sum(jnp.where(oh0, row, 0.0), axis=0, keepdims=True)
    row1 = jnp.sum(jnp.where(oh1, row, 0.0), axis=0, keepdims=True)
    cnt_ref[...] = count
    ri_ref[...] = jnp.zeros(ri_ref.shape, jnp.int32)
    ri_ref[0:1, :] = i0
    ri_ref[1:2, :] = i1
    ri_ref[2:3, :] = row0.astype(jnp.int32)
    ri_ref[3:4, :] = row1.astype(jnp.int32)
    rw_ref[...] = jnp.zeros(rw_ref.shape, F32)
    rw_ref[0:1, :] = w0
    rw_ref[1:2, :] = w1


def _route_dispatch_kernel(x_ref, wr_ref, ri_ref, rw_ref, cnt_ref, o_ref):
    _router_kernel(x_ref, wr_ref, ri_ref, rw_ref, cnt_ref)
    rows, tb = o_ref.shape[0], x_ref.shape[0]
    r = lax.broadcasted_iota(jnp.int32, (rows, tb), 0)
    sel = jnp.logical_or(r == ri_ref[2:3, :], r == ri_ref[3:4, :])
    perm = jnp.where(sel, 1.0, 0.0).astype(BF16)
    o_ref[...] = jnp.dot(perm, x_ref[...].astype(BF16), preferred_element_type=F32).astype(BF16)


def _route_dispatch(x, wr_t):
    t, d = x.shape
    nb = t // MOE_BLOCK
    return pl.pallas_call(
        _route_dispatch_kernel,
        grid=(nb,),
        in_specs=[pl.BlockSpec((MOE_BLOCK, d), lambda i: (i, 0)),
                  pl.BlockSpec((N_EXPERTS, d), lambda i: (0, 0))],
        out_specs=[pl.BlockSpec((SUBLANE, MOE_BLOCK), lambda i: (0, i)),
                   pl.BlockSpec((SUBLANE, MOE_BLOCK), lambda i: (0, i)),
                   pl.BlockSpec((None, N_EXPERTS, LANE), lambda i: (i, 0, 0)),
                   pl.BlockSpec((LOC_ROWS, d), lambda i: (i, 0))],
        out_shape=[jax.ShapeDtypeStruct((SUBLANE, t), jnp.int32),
                   jax.ShapeDtypeStruct((SUBLANE, t), F32),
                   jax.ShapeDtypeStruct((nb, N_EXPERTS, LANE), F32),
                   jax.ShapeDtypeStruct((nb * LOC_ROWS, d), BF16)],
        compiler_params=_params("parallel"),
        name="route_dispatch",
    )(x, wr_t)


def _experts_kernel(te_ref, nu_ref, gran_ref, x_hbm, wg_hbm, wu_hbm, wd_hbm, yinit_hbm, y_hbm,
                    xbuf, ybuf, acc_ref, wg_buf, wu_buf, wd_buf, sem_in, sem_out, sem_w):
    del yinit_hbm
    i = pl.program_id(0)
    n_tiles = pl.num_programs(0)
    n_used = nu_ref[0]
    n_chunks = wg_buf.shape[0]
    expert = te_ref[i]
    prev_expert = te_ref[jnp.maximum(i - 1, 0)]
    next_expert = te_ref[jnp.minimum(i + 1, n_tiles - 1)]
    first_of_expert = jnp.logical_or(i == 0, prev_expert != expert)
    refill = jnp.logical_and(i + 1 < n_used, next_expert != expert)

    def weight_copies(e, n):
        cols = pl.ds(n * MOE_FF_TILE, MOE_FF_TILE)
        return (pltpu.make_async_copy(wg_hbm.at[e, :, cols], wg_buf.at[n], sem_w.at[n]),
                pltpu.make_async_copy(wu_hbm.at[e, :, cols], wu_buf.at[n], sem_w.at[n]),
                pltpu.make_async_copy(wd_hbm.at[e, cols, :], wd_buf.at[n], sem_w.at[n]))

    def for_granules(tile, fn):
        def body(s, carry):
            g = gran_ref[tile * GRAN_PER_TILE + s]
            fn(pl.ds(pl.multiple_of(g * GRANULE, GRANULE), GRANULE),
               pl.ds(pl.multiple_of(s * GRANULE, GRANULE), GRANULE))
            return carry
        lax.fori_loop(0, nu_ref[1 + tile], body, 0)

    def gather(tile, slot):
        return lambda src, dst: pltpu.make_async_copy(x_hbm.at[src], xbuf.at[slot, dst],
                                                      sem_in.at[slot])

    def scatter(src, dst):
        return pltpu.make_async_copy(ybuf.at[dst], y_hbm.at[src], sem_out.at[0])

    @pl.when(i < n_used)
    def _():
        slot = i % 2

        @pl.when(i == 0)
        def _():
            for n in range(n_chunks):
                for cp in weight_copies(expert, n):
                    cp.start()
            xbuf[...] = jnp.zeros(xbuf.shape, BF16)
            for_granules(0, lambda s, d: gather(0, 0)(s, d).start())

        for_granules(i, lambda s, d: gather(i, slot)(s, d).wait())

        @pl.when(i + 1 < n_used)
        def _():
            for_granules(i + 1, lambda s, d: gather(i + 1, 1 - slot)(s, d).start())

        xb = xbuf[slot]
        for n in range(n_chunks):
            @pl.when(first_of_expert)
            def _():
                for cp in weight_copies(expert, n):
                    cp.wait()

            part = _swiglu_step(xb, wg_buf.at[n], wu_buf.at[n], wd_buf.at[n])
            if n == 0:
                acc_ref[...] = part
            else:
                acc_ref[...] += part

            @pl.when(refill)
            def _():
                for cp in weight_copies(next_expert, n):
                    cp.start()

        @pl.when(i > 0)
        def _():
            for_granules(i - 1, lambda s, d: scatter(s, d).wait())

        ybuf[...] = acc_ref[...].astype(BF16)
        for_granules(i, lambda s, d: scatter(s, d).start())

        @pl.when(i == n_used - 1)
        def _():
            for_granules(i, lambda s, d: scatter(s, d).wait())


def _experts(tile_expert, n_used, gran, x_loc, wg, wu, wd):
    rows, d = x_loc.shape
    ff = wg.shape[2]
    n_tiles = tile_expert.shape[0]
    nj = ff // MOE_FF_TILE
    hbm = pl.BlockSpec(memory_space=pl.ANY)
    return pl.pallas_call(
        _experts_kernel,
        grid_spec=pltpu.PrefetchScalarGridSpec(
            num_scalar_prefetch=3,
            grid=(n_tiles,),
            in_specs=[hbm, hbm, hbm, hbm, hbm],
            out_specs=hbm,
            scratch_shapes=[pltpu.VMEM((2, FFN_ROW_TILE, d), BF16),
                            pltpu.VMEM((FFN_ROW_TILE, d), BF16),
                            pltpu.VMEM((FFN_ROW_TILE, d), F32),
                            pltpu.VMEM((nj, d, MOE_FF_TILE), BF16),
                            pltpu.VMEM((nj, d, MOE_FF_TILE), BF16),
                            pltpu.VMEM((nj, MOE_FF_TILE, d), BF16),
                            pltpu.SemaphoreType.DMA((2,)),
                            pltpu.SemaphoreType.DMA((1,)),
                            pltpu.SemaphoreType.DMA((nj,))]),
        out_shape=jax.ShapeDtypeStruct((rows, d), BF16),
        input_output_aliases={7: 0},
        compiler_params=_params("arbitrary"),
        name="swiglu_experts",
    )(tile_expert, n_used, gran, x_loc, wg, wu, wd, jnp.zeros((rows, d), BF16))


def _combine_ln_kernel(y_ref, row_ref, w_ref, x_ref, g_ref, b_ref, o_ref):
    tb, rows = x_ref.shape[0], y_ref.shape[0]
    c = lax.broadcasted_iota(jnp.int32, (tb, rows), 1)
    q = (jnp.where(c == row_ref[:, 0:1], w_ref[:, 0:1], 0.0)
         + jnp.where(c == row_ref[:, 1:2], w_ref[:, 1:2], 0.0))
    f = jnp.dot(q.astype(BF16), y_ref[...], preferred_element_type=F32)
    o_ref[...] = _layer_norm(DEEPNORM_ALPHA * x_ref[...] + f, g_ref[...], b_ref[...])


def _combine_ln(y_loc, row_col, w_col, x, g, b):
    t, d = x.shape
    return pl.pallas_call(
        _combine_ln_kernel,
        grid=(t // MOE_BLOCK,),
        in_specs=[pl.BlockSpec((LOC_ROWS, d), lambda i: (i, 0)),
                  pl.BlockSpec((MOE_BLOCK, TOP_K), lambda i: (i, 0)),
                  pl.BlockSpec((MOE_BLOCK, TOP_K), lambda i: (i, 0)),
                  pl.BlockSpec((MOE_BLOCK, d), lambda i: (i, 0)),
                  pl.BlockSpec((1, d), lambda i: (0, 0)),
                  pl.BlockSpec((1, d), lambda i: (0, 0))],
        out_specs=pl.BlockSpec((MOE_BLOCK, d), lambda i: (i, 0)),
        out_shape=jax.ShapeDtypeStruct((t, d), F32),
        compiler_params=_params("parallel"),
        name="combine_ln",
    )(y_loc, row_col, w_col, x, g, b)


def _mixer(x, batch, seq, w_in, gm_ws, gm_bs, gm_ln_g, gm_ln_b, gla_wa2, gla_ba, gla_norm_g,
           w_out, ln_g, ln_b):
    w_gate = _fold_gate(w_in[:, PROJ_MAIN:], gla_wa2)
    w_ext = jnp.concatenate([w_in[:, :PROJ_MAIN], w_gate], axis=1).astype(BF16)
    del batch
    return _mixer_call(x, w_ext, w_out.astype(BF16), ln_g.reshape(1, -1), ln_b.reshape(1, -1),
                       gm_ws, gm_bs.reshape(GM_HEADS, GM_CHUNK, 1),
                       gm_ln_g.reshape(1, -1), gm_ln_b.reshape(1, -1),
                       gla_ba.reshape(1, -1), gla_norm_g.reshape(1, -1), seq)


def _granule_table(counts):
    nb = counts.shape[0]
    n_tiles = (nb * (TOP_K * MOE_BLOCK + N_EXPERTS * (GRANULE - 1))) // FFN_ROW_TILE + N_EXPERTS
    gran_per_block = LOC_ROWS // GRANULE
    g = (counts + GRANULE - 1) // GRANULE
    seg_start = jnp.cumsum(g, axis=1) - g
    cum_incl = jnp.cumsum(g, axis=0)
    cum_excl = cum_incl - g
    total = cum_incl[-1]
    tiles_per_expert = (total + GRAN_PER_TILE - 1) // GRAN_PER_TILE
    tile_end = jnp.cumsum(tiles_per_expert)
    n_used = tile_end[-1:]
    tile = jnp.arange(n_tiles, dtype=jnp.int32)
    tile_expert = jnp.minimum(jnp.sum(tile[:, None] >= tile_end[None, :], axis=1), N_EXPERTS - 1)
    pick = (tile_expert[:, None] == jnp.arange(N_EXPERTS, dtype=jnp.int32)[None, :]).astype(jnp.int32)
    per_tile = lambda tab: jnp.sum(pick[:, :, None] * tab.T[None, :, :], axis=1)
    first_tile = jnp.sum(pick * (tile_end - tiles_per_expert)[None, :], axis=1)
    vg = ((tile - first_tile)[:, None] * GRAN_PER_TILE
          + jnp.arange(GRAN_PER_TILE, dtype=jnp.int32)[None, :])[:, :, None]
    lo = per_tile(cum_excl)[:, None, :]
    hi = per_tile(cum_incl)[:, None, :]
    base = (jnp.arange(nb, dtype=jnp.int32)[None, :] * gran_per_block + per_tile(seg_start))[:, None, :]
    inside = jnp.logical_and(vg >= lo, vg < hi)
    phys = jnp.sum(jnp.where(inside, base + vg - lo, 0), axis=2)
    valid = jnp.logical_and(jnp.any(inside, axis=2), (tile < n_used[0])[:, None])
    gran = jnp.where(valid, phys, 0).astype(jnp.int32).reshape(-1)
    used = jnp.concatenate([n_used, jnp.sum(valid, axis=1)]).astype(jnp.int32)
    return tile_expert.astype(jnp.int32), used, gran


def _moe_ffn(x, w_router, e_gate, e_up, e_down, ln_g, ln_b):
    ri, rw, cnt, x_loc = _route_dispatch(x, w_router.T)
    tile_expert, n_used, gran = _granule_table(cnt[:, :, 0].astype(jnp.int32))
    y_loc = _experts(tile_expert, n_used, gran, x_loc,
                     e_gate.astype(BF16), e_up.astype(BF16), e_down.astype(BF16))
    return _combine_ln(y_loc, ri[2:2 + TOP_K].T, rw[:TOP_K].T, x,
                       ln_g.reshape(1, -1), ln_b.reshape(1, -1))


def kernel(x, w_in, gm_ws, gm_bs, gm_ln_g, gm_ln_b, gla_wa2, gla_ba, gla_norm_g, w_out, ln_mix_g, ln_mix_b, ffn_w_gate, ffn_w_up, ffn_w_down, router_w, exp_w_gate, exp_w_up, exp_w_down, ln_ffn_g, ln_ffn_b):
    batch, seq, d = x.shape
    h = x.reshape(batch * seq, d)
    for layer in range(DEPTH):
        h = _mixer(h, batch, seq, w_in[layer], gm_ws[layer], gm_bs[layer], gm_ln_g[layer],
                   gm_ln_b[layer], gla_wa2[layer], gla_ba[layer], gla_norm_g[layer], w_out[layer],
                   ln_mix_g[layer], ln_mix_b[layer])
        i = layer // 2
        if layer % 2 == 0:
            h = _swiglu_ln(h, ffn_w_gate[i].astype(BF16), ffn_w_up[i].astype(BF16),
                           ffn_w_down[i].astype(BF16), ln_ffn_g[layer].reshape(1, -1),
                           ln_ffn_b[layer].reshape(1, -1))
        else:
            h = _moe_ffn(h, router_w[i], exp_w_gate[i], exp_w_up[i], exp_w_down[i],
                         ln_ffn_g[layer], ln_ffn_b[layer])
    return h.reshape(batch, seq, d)
```

```python
import functools

import jax
import jax.numpy as jnp
from jax import lax
from jax.experimental import pallas as pl
from jax.experimental.pallas import tpu as pltpu

F32 = jnp.float32
BF16 = jnp.bfloat16

D_MODEL = 1024
DEPTH = 2
GM_HEADS = 4
GM_WIDTH = 512
GM_HEAD_DIM = 128
GM_CHUNK = 128
GLA_HEADS = 4
GLA_WIDTH = 512
GLA_DV = 128
GLA_KEY_WIDTH = 256
GLA_DK = 64
GLA_GATE_RANK = 16
GLA_TAU = 16.0
GLA_CHUNK = 64
PROJ_MAIN = 2 * GM_WIDTH + 2 * GLA_KEY_WIDTH + 2 * GLA_WIDTH
N_EXPERTS = 8
TOP_K = 2
LN_EPS = 1e-5
RMS_EPS = 1e-6
DEEPNORM_ALPHA = (2 * DEPTH) ** 0.25

V7X_VMEM_LIMIT_BYTES = 56 * 1024 * 1024
LANE = 128
SUBLANE = 8
MXU_N = 256

ROW_TILE = 512
FFN_ROW_TILE = 1024
DENSE_FF_TILE = 256
MOE_FF_TILE = 512
MOE_BLOCK = 512
GRANULE = 16
LOC_ROWS = -(-(TOP_K * MOE_BLOCK + N_EXPERTS * (GRANULE - 1)) // LANE) * LANE
GRAN_PER_TILE = FFN_ROW_TILE // GRANULE


def _params(*sem, **kw):
    return pltpu.CompilerParams(dimension_semantics=sem, vmem_limit_bytes=V7X_VMEM_LIMIT_BYTES, **kw)


def _gelu(x):
    return 0.5 * x * (1.0 + lax.erf(x * (2.0 ** -0.5)))


def _layer_norm(y, g, b):
    mu = jnp.mean(y, axis=-1, keepdims=True)
    d = y - mu
    var = jnp.mean(d * d, axis=-1, keepdims=True)
    return d * lax.rsqrt(var + LN_EPS) * g + b


def _fold_kernel(w1_ref, w2_ref, o_ref):
    acc = jnp.zeros(o_ref.shape, F32)
    for r in range(GLA_GATE_RANK):
        acc = acc + w1_ref[:, r:r + 1] * w2_ref[r:r + 1, :]
    o_ref[...] = acc


def _fold_gate(w1, w2):
    return pl.pallas_call(
        _fold_kernel,
        out_shape=jax.ShapeDtypeStruct((w1.shape[0], w2.shape[1]), F32),
        name="fold_gate",
    )(w1, w2)


def _inproj_items(x_ref, w_ref, p_ref, z_ref):
    cache = {}

    def xb():
        if "xb" not in cache:
            cache["xb"] = x_ref[...].astype(BF16)
        return cache["xb"]

    def block(n):
        cols = slice(n * MXU_N, (n + 1) * MXU_N)

        def run():
            p_ref[:, cols] = jnp.dot(xb(), w_ref[:, cols], preferred_element_type=F32).astype(BF16)
        return run

    def gate_logits():
        z_ref[...] = jnp.dot(xb(), w_ref[:, PROJ_MAIN:], preferred_element_type=F32)

    return [block(n) for n in range(PROJ_MAIN // MXU_N)] + [gate_logits]


def _gmlp_items(uv_ref, w_ref, bcol_ref, g_ref, b_ref, o_ref):
    rows = uv_ref.shape[0]

    def head(h):
        cols = slice(h * GM_HEAD_DIM, (h + 1) * GM_HEAD_DIM)
        vcols = slice(GM_WIDTH + h * GM_HEAD_DIM, GM_WIDTH + (h + 1) * GM_HEAD_DIM)

        def run():
            r = lax.broadcasted_iota(jnp.int32, (GM_CHUNK, GM_CHUNK), 0)
            c = lax.broadcasted_iota(jnp.int32, (GM_CHUNK, GM_CHUNK), 1)
            wm = jnp.where(r >= c, w_ref[h], 0.0).astype(BF16)
            v = _gelu(uv_ref[:, vcols].astype(F32))
            vn = _layer_norm(v, g_ref[:, cols], b_ref[:, cols]).astype(BF16)
            u = _gelu(uv_ref[:, cols].astype(F32))
            for ci in range(rows // GM_CHUNK):
                rs = slice(ci * GM_CHUNK, (ci + 1) * GM_CHUNK)
                s = jnp.dot(wm, vn[rs], preferred_element_type=F32) + bcol_ref[h]
                o_ref[rs, cols] = (u[rs] * s).astype(BF16)
        return run

    return [head(h) for h in range(GM_HEADS)]


def _gla_items(q_ref, k_ref, v_ref, g_ref, z_ref, ba_ref, ng_ref, o_ref,
               st_ref, qd_ref, ki_ref, kd_ref, dec_ref, raw_ref):
    rows = q_ref.shape[0]
    n_chunks = rows // GLA_CHUNK
    shift = GLA_CHUNK.bit_length() - 1
    nt = (((1,), (1,)), ((), ()))
    tn = (((0,), (0,)), ((), ()))

    def prep():
        r = lax.broadcasted_iota(jnp.int32, (rows, rows), 0)
        c = lax.broadcasted_iota(jnp.int32, (rows, rows), 1)
        ones_tril = jnp.where(jnp.logical_and((r >> shift) == (c >> shift), r >= c),
                              1.0, 0.0).astype(BF16)
        z = z_ref[...] + ba_ref[...]
        la = (jnp.minimum(z, 0.0) - jnp.log1p(jnp.exp(-jnp.abs(z)))) * (1.0 / GLA_TAU)
        la_hi = la.astype(BF16)
        la_lo = (la - la_hi.astype(F32)).astype(BF16)
        b = (jnp.dot(ones_tril, la_hi, preferred_element_type=F32)
             + jnp.dot(ones_tril, la_lo, preferred_element_type=F32))
        b3 = b.reshape(n_chunks, GLA_CHUNK, GLA_KEY_WIDTH)
        b_last = b3[:, GLA_CHUNK - 1:GLA_CHUNK, :]
        k = k_ref[...].astype(F32)
        qd_ref[...] = (q_ref[...].astype(F32) * (GLA_DK ** -0.5) * jnp.exp(b)).astype(BF16)
        ki_ref[...] = (k * jnp.exp(-b)).astype(BF16)
        kd_ref[...] = (k * jnp.exp(b_last - b3).reshape(rows, GLA_KEY_WIDTH)).astype(BF16)
        dec_ref[...] = jnp.exp(b_last).reshape(n_chunks, GLA_KEY_WIDTH)

    def chunk(ci):
        rs = slice(ci * GLA_CHUNK, (ci + 1) * GLA_CHUNK)

        def run():
            rc = lax.broadcasted_iota(jnp.int32, (GLA_CHUNK, GLA_CHUNK), 0)
            cc = lax.broadcasted_iota(jnp.int32, (GLA_CHUNK, GLA_CHUNK), 1)
            causal = rc >= cc
            q_dec = qd_ref[rs, :]
            k_inv = ki_ref[rs, :]
            k_dec = kd_ref[rs, :]
            decay = dec_ref[ci:ci + 1, :]
            for h in range(GLA_HEADS):
                ks = slice(h * GLA_DK, (h + 1) * GLA_DK)
                vs = slice(h * GLA_DV, (h + 1) * GLA_DV)
                scores = lax.dot_general(q_dec[:, ks], k_inv[:, ks], nt, preferred_element_type=F32)
                scores = jnp.where(causal, scores, 0.0).astype(BF16)
                vh = v_ref[rs, vs]
                state_t = st_ref[h]
                o = (jnp.dot(scores, vh, preferred_element_type=F32)
                     + lax.dot_general(q_dec[:, ks], state_t.astype(BF16), nt,
                                       preferred_element_type=F32))
                st_ref[h] = (state_t * decay[:, ks]
                             + lax.dot_general(vh, k_dec[:, ks], tn, preferred_element_type=F32))
                raw_ref[rs, vs] = o
        return run

    def finish():
        for h in range(GLA_HEADS):
            vs = slice(h * GLA_DV, (h + 1) * GLA_DV)
            o = raw_ref[:, vs]
            o = o * lax.rsqrt(jnp.mean(o * o, axis=-1, keepdims=True) + RMS_EPS) * ng_ref[:, vs]
            gate = g_ref[:, vs].astype(F32)
            o_ref[:, vs] = (o * (gate * jax.nn.sigmoid(gate))).astype(BF16)

    return prep, [chunk(ci) for ci in range(n_chunks)], finish


def _mixer_kernel(xa_ref, xc_ref, win_ref, wout_ref, lng_ref, lnb_ref,
                  gw_ref, gbcol_ref, gg_ref, gb_ref, ba_ref, ng_ref, o_ref,
                  proj0, proj1, z0, z1, ab0, ab1, st_ref, *gla_scratch, tiles_per_seq):
    t = pl.program_id(0)
    proj_scr, z_scr, ab_scr = (proj0, proj1), (z0, z1), (ab0, ab1)

    @pl.when(t == 0)
    def _():
        for ref in (proj0, proj1, z0, z1, ab0, ab1, st_ref):
            ref[...] = jnp.zeros(ref.shape, ref.dtype)

    @pl.when(lax.rem(t + tiles_per_seq - 1, tiles_per_seq) == 0)
    def _():
        st_ref[...] = jnp.zeros(st_ref.shape, F32)

    q0 = 2 * GM_WIDTH
    k0 = q0 + GLA_KEY_WIDTH
    v0 = k0 + GLA_KEY_WIDTH
    g0 = v0 + GLA_WIDTH

    def stages(proj_a, z_a, proj_b, z_b, ab_b, ab_c):
        inproj = _inproj_items(xa_ref, win_ref, proj_a, z_a)
        gla_prep, gla_chunks, gla_finish = _gla_items(
            proj_b.at[:, q0:k0], proj_b.at[:, k0:v0], proj_b.at[:, v0:g0],
            proj_b.at[:, g0:PROJ_MAIN], z_b, ba_ref, ng_ref, ab_b.at[:, GM_WIDTH:],
            st_ref, *gla_scratch)
        gmlp = _gmlp_items(proj_b.at[:, :q0], gw_ref, gbcol_ref, gg_ref, gb_ref,
                           ab_b.at[:, :GM_WIDTH])
        outproj, outproj_finish = _outproj_items(ab_c.at[:, :GM_WIDTH], ab_c.at[:, GM_WIDTH:],
                                                 wout_ref, xc_ref, lng_ref, lnb_ref, o_ref)
        for run in inproj + [gla_prep] + gla_chunks + [gla_finish] + gmlp + outproj + [outproj_finish]:
            run()

    for parity in range(2):
        @pl.when(t % 2 == parity)
        def _(a=parity, b=1 - parity):
            stages(proj_scr[a], z_scr[a], proj_scr[b], z_scr[b], ab_scr[b], ab_scr[a])


def _mixer_call(x, w_ext, w_out, ln_g, ln_b, gm_ws, gm_bcol, gm_ln_g, gm_ln_b, ba, norm_g, seq):
    t, d = x.shape
    n_tiles = t // ROW_TILE
    kw, vw = GLA_KEY_WIDTH, GLA_WIDTH
    const2 = lambda i: (0, 0)
    const3 = lambda i: (0, 0, 0)
    resident = pl.Buffered(1)
    tile_c = lambda i: (jnp.maximum(i - 2, 0), 0)
    return pl.pallas_call(
        functools.partial(_mixer_kernel, tiles_per_seq=seq // ROW_TILE),
        grid=(n_tiles + 2,),
        in_specs=[pl.BlockSpec((ROW_TILE, d), lambda i: (jnp.minimum(i, n_tiles - 1), 0)),
                  pl.BlockSpec((ROW_TILE, d), tile_c),
                  pl.BlockSpec(w_ext.shape, const2, pipeline_mode=resident),
                  pl.BlockSpec(w_out.shape, const2, pipeline_mode=resident),
                  pl.BlockSpec((1, d), const2),
                  pl.BlockSpec((1, d), const2),
                  pl.BlockSpec((GM_HEADS, GM_CHUNK, GM_CHUNK), const3),
                  pl.BlockSpec((GM_HEADS, GM_CHUNK, 1), const3),
                  pl.BlockSpec((1, GM_WIDTH), const2),
                  pl.BlockSpec((1, GM_WIDTH), const2),
                  pl.BlockSpec((1, kw), const2),
                  pl.BlockSpec((1, vw), const2)],
        out_specs=pl.BlockSpec((ROW_TILE, d), tile_c),
        out_shape=jax.ShapeDtypeStruct((t, d), F32),
        scratch_shapes=[pltpu.VMEM((ROW_TILE, PROJ_MAIN), BF16),
                        pltpu.VMEM((ROW_TILE, PROJ_MAIN), BF16),
                        pltpu.VMEM((ROW_TILE, kw), F32),
                        pltpu.VMEM((ROW_TILE, kw), F32),
                        pltpu.VMEM((ROW_TILE, GM_WIDTH + vw), BF16),
                        pltpu.VMEM((ROW_TILE, GM_WIDTH + vw), BF16),
                        pltpu.VMEM((GLA_HEADS, GLA_DV, GLA_DK), F32),
                        pltpu.VMEM((ROW_TILE, kw), BF16),
                        pltpu.VMEM((ROW_TILE, kw), BF16),
                        pltpu.VMEM((ROW_TILE, kw), BF16),
                        pltpu.VMEM((ROW_TILE // GLA_CHUNK, kw), F32),
                        pltpu.VMEM((ROW_TILE, vw), F32)],
        compiler_params=_params("arbitrary"),
        name="mixer",
    )(x, x, w_ext, w_out, ln_g, ln_b, gm_ws, gm_bcol, gm_ln_g, gm_ln_b, ba, norm_g)


def _outproj_items(a_ref, b_ref, w_ref, x_ref, g_ref, be_ref, o_ref):
    def block(n):
        cols = slice(n * MXU_N, (n + 1) * MXU_N)

        def run():
            h = (jnp.dot(a_ref[...], w_ref[:GM_WIDTH, cols], preferred_element_type=F32)
                 + jnp.dot(b_ref[...], w_ref[GM_WIDTH:, cols], preferred_element_type=F32))
            o_ref[:, cols] = DEEPNORM_ALPHA * x_ref[:, cols] + h
        return run

    def finish():
        o_ref[...] = _layer_norm(o_ref[...], g_ref[...], be_ref[...])

    return [block(n) for n in range(o_ref.shape[1] // MXU_N)], finish


def _swiglu_ln_kernel(x_ref, wg_ref, wu_ref, wd_ref, g_ref, b_ref, o_ref, hid_ref):
    xb = x_ref[...].astype(BF16)
    for n in range(wg_ref.shape[1] // DENSE_FF_TILE):
        cols = slice(n * DENSE_FF_TILE, (n + 1) * DENSE_FF_TILE)
        gate = jnp.dot(xb, wg_ref[:, cols], preferred_element_type=F32)
        up = jnp.dot(xb, wu_ref[:, cols], preferred_element_type=F32)
        hid_ref[:, cols] = (gate * jax.nn.sigmoid(gate) * up).astype(BF16)
    f = jnp.dot(hid_ref[...], wd_ref[...], preferred_element_type=F32)
    o_ref[...] = _layer_norm(DEEPNORM_ALPHA * x_ref[...] + f, g_ref[...], b_ref[...])


def _swiglu_ln(x, wg, wu, wd, g, b):
    rows, d = x.shape
    ff = wg.shape[1]
    resident = pl.Buffered(1)
    return pl.pallas_call(
        _swiglu_ln_kernel,
        grid=(rows // FFN_ROW_TILE,),
        in_specs=[pl.BlockSpec((FFN_ROW_TILE, d), lambda i: (i, 0)),
                  pl.BlockSpec((d, ff), lambda i: (0, 0), pipeline_mode=resident),
                  pl.BlockSpec((d, ff), lambda i: (0, 0), pipeline_mode=resident),
                  pl.BlockSpec((ff, d), lambda i: (0, 0), pipeline_mode=resident),
                  pl.BlockSpec((1, d), lambda i: (0, 0)),
                  pl.BlockSpec((1, d), lambda i: (0, 0))],
        out_specs=pl.BlockSpec((FFN_ROW_TILE, d), lambda i: (i, 0)),
        out_shape=jax.ShapeDtypeStruct((rows, d), F32),
        scratch_shapes=[pltpu.VMEM((FFN_ROW_TILE, ff), BF16)],
        compiler_params=_params("parallel"),
        name="swiglu_ln",
    )(x, wg, wu, wd, g, b)


def _split_bf16(v):
    hi = v.astype(BF16)
    return hi, (v - hi.astype(F32)).astype(BF16)


def _router_kernel(x_ref, wr_ref, ri_ref, rw_ref, cnt_ref):
    tb = x_ref.shape[0]
    x_hi, x_lo = _split_bf16(x_ref[...])
    w_hi, w_lo = _split_bf16(wr_ref[...])
    nt = (((1,), (1,)), ((), ()))
    logits = (lax.dot_general(w_hi, x_hi, nt, preferred_element_type=F32)
              + lax.dot_general(w_hi, x_lo, nt, preferred_element_type=F32)
              + lax.dot_general(w_lo, x_hi, nt, preferred_element_type=F32))
    e_iota = lax.broadcasted_iota(jnp.int32, (N_EXPERTS, tb), 0)
    m0 = jnp.max(logits, axis=0, keepdims=True)
    i0 = jnp.min(jnp.where(logits == m0, e_iota, N_EXPERTS), axis=0, keepdims=True)
    rest = jnp.where(e_iota == i0, -jnp.inf, logits)
    m1 = jnp.max(rest, axis=0, keepdims=True)
    i1 = jnp.min(jnp.where(rest == m1, e_iota, N_EXPERTS), axis=0, keepdims=True)
    ex = jnp.exp(m1 - m0)
    w0 = 1.0 / (1.0 + ex)
    w1 = ex / (1.0 + ex)
    oh0 = e_iota == i0
    oh1 = e_iota == i1
    onehot = jnp.where(jnp.logical_or(oh0, oh1), 1.0, 0.0)
    tr = lax.broadcasted_iota(jnp.int32, (tb, tb), 0)
    tc = lax.broadcasted_iota(jnp.int32, (tb, tb), 1)
    before = jnp.where(tr < tc, 1.0, 0.0).astype(BF16)
    rank = jnp.dot(onehot.astype(BF16), before, preferred_element_type=F32)
    count = jnp.broadcast_to(jnp.sum(onehot, axis=1, keepdims=True), (N_EXPERTS, LANE))
    cap = jnp.ceil(count * (1.0 / GRANULE)) * GRANULE
    e_sub = lax.broadcasted_iota(jnp.int32, (N_EXPERTS, LANE), 0)
    seg = jnp.zeros((N_EXPERTS, LANE), F32)
    for e in range(N_EXPERTS - 1):
        seg = seg + jnp.where(e_sub > e, cap[e:e + 1, :], 0.0)
    row = rank + seg[:, 0:1]
    row0 = jnp.sum(jnp.where(oh0, row, 0.0), axis=0, keepdims=True)
    row1 = jnp.sum(jnp.where(oh1, row, 0.0), axis=0, keepdims=True)
    cnt_ref[...] = count
    ri_ref[...] = jnp.zeros(ri_ref.shape, jnp.int32)
    ri_ref[0:1, :] = i0
    ri_ref[1:2, :] = i1
    ri_ref[2:3, :] = row0.astype(jnp.int32)
    ri_ref[3:4, :] = row1.astype(jnp.int32)
    rw_ref[...] = jnp.zeros(rw_ref.shape, F32)
    rw_ref[0:1, :] = w0
    rw_ref[1:2, :] = w1
    return x_hi


def _route_dispatch_kernel(x_ref, wr_ref, ri_ref, rw_ref, cnt_ref, o_ref):
    xb = _router_kernel(x_ref, wr_ref, ri_ref, rw_ref, cnt_ref)
    rows, tb = o_ref.shape[0], x_ref.shape[0]
    r = lax.broadcasted_iota(jnp.int32, (rows, tb), 0)
    sel = jnp.logical_or(r == ri_ref[2:3, :], r == ri_ref[3:4, :])
    perm = jnp.where(sel, 1.0, 0.0).astype(BF16)
    o_ref[...] = jnp.dot(perm, xb, preferred_element_type=F32).astype(BF16)


def _route_dispatch(x, wr_t):
    t, d = x.shape
    nb = t // MOE_BLOCK
    return pl.pallas_call(
        _route_dispatch_kernel,
        grid=(nb,),
        in_specs=[pl.BlockSpec((MOE_BLOCK, d), lambda i: (i, 0)),
                  pl.BlockSpec((N_EXPERTS, d), lambda i: (0, 0))],
        out_specs=[pl.BlockSpec((SUBLANE, MOE_BLOCK), lambda i: (0, i)),
                   pl.BlockSpec((SUBLANE, MOE_BLOCK), lambda i: (0, i)),
                   pl.BlockSpec((None, N_EXPERTS, LANE), lambda i: (i, 0, 0)),
                   pl.BlockSpec((LOC_ROWS, d), lambda i: (i, 0))],
        out_shape=[jax.ShapeDtypeStruct((SUBLANE, t), jnp.int32),
                   jax.ShapeDtypeStruct((SUBLANE, t), F32),
                   jax.ShapeDtypeStruct((nb, N_EXPERTS, LANE), F32),
                   jax.ShapeDtypeStruct((nb * LOC_ROWS, d), BF16)],
        compiler_params=_params("parallel"),
        name="route_dispatch",
    )(x, wr_t)


def _experts_kernel(te_ref, nu_ref, gran_ref, x_hbm, wg_hbm, wu_hbm, wd_hbm, y_hbm,
                    xbuf, ybuf, hid_ref, wg_buf, wu_buf, wd_buf, sem_in, sem_out, sem_w):
    i = pl.program_id(0)
    n_tiles = pl.num_programs(0)
    n_used = nu_ref[0]
    n_chunks = wg_buf.shape[0]
    expert = te_ref[i]
    prev_expert = te_ref[jnp.maximum(i - 1, 0)]
    next_expert = te_ref[jnp.minimum(i + 1, n_tiles - 1)]
    first_of_expert = jnp.logical_or(i == 0, prev_expert != expert)
    refill = jnp.logical_and(i + 1 < n_used, next_expert != expert)

    def up_copies(e, n):
        cols = pl.ds(n * MOE_FF_TILE, MOE_FF_TILE)
        return (pltpu.make_async_copy(wg_hbm.at[e, :, cols], wg_buf.at[n], sem_w.at[n]),
                pltpu.make_async_copy(wu_hbm.at[e, :, cols], wu_buf.at[n], sem_w.at[n]))

    def down_copy(e):
        return pltpu.make_async_copy(wd_hbm.at[e], wd_buf, sem_w.at[n_chunks])

    def for_granules(tile, fn):
        def body(s, carry):
            g = gran_ref[tile * GRAN_PER_TILE + s]
            fn(pl.ds(pl.multiple_of(g * GRANULE, GRANULE), GRANULE),
               pl.ds(pl.multiple_of(s * GRANULE, GRANULE), GRANULE))
            return carry
        lax.fori_loop(0, nu_ref[1 + tile], body, 0)

    def gather(slot):
        return lambda hbm_rows, tile_rows: pltpu.make_async_copy(
            x_hbm.at[hbm_rows], xbuf.at[slot, tile_rows], sem_in.at[slot])

    def scatter(hbm_rows, tile_rows):
        return pltpu.make_async_copy(ybuf.at[tile_rows], y_hbm.at[hbm_rows], sem_out.at[0])

    @pl.when(i < n_used)
    def _():
        slot = i % 2

        @pl.when(i == 0)
        def _():
            for n in range(n_chunks):
                for cp in up_copies(expert, n):
                    cp.start()
            down_copy(expert).start()
            xbuf[...] = jnp.zeros(xbuf.shape, BF16)
            for_granules(0, lambda h, r: gather(0)(h, r).start())

        for_granules(i, lambda h, r: gather(slot)(h, r).wait())

        @pl.when(i + 1 < n_used)
        def _():
            for_granules(i + 1, lambda h, r: gather(1 - slot)(h, r).start())

        xb = xbuf[slot]
        for n in range(n_chunks):
            @pl.when(first_of_expert)
            def _():
                for cp in up_copies(expert, n):
                    cp.wait()

            gate = jnp.dot(xb, wg_buf[n], preferred_element_type=F32)
            up = jnp.dot(xb, wu_buf[n], preferred_element_type=F32)
            hid_ref[:, n * MOE_FF_TILE:(n + 1) * MOE_FF_TILE] = (
                gate * jax.nn.sigmoid(gate) * up).astype(BF16)

            @pl.when(refill)
            def _():
                for cp in up_copies(next_expert, n):
                    cp.start()

        @pl.when(first_of_expert)
        def _():
            down_copy(expert).wait()

        @pl.when(i > 0)
        def _():
            for_granules(i - 1, lambda h, r: scatter(h, r).wait())

        ybuf[...] = jnp.dot(hid_ref[...], wd_buf[...], preferred_element_type=F32).astype(BF16)

        @pl.when(refill)
        def _():
            down_copy(next_expert).start()

        for_granules(i, lambda h, r: scatter(h, r).start())

        @pl.when(i == n_used - 1)
        def _():
            for_granules(i, lambda h, r: scatter(h, r).wait())


def _experts(tile_expert, used, gran, x_loc, wg, wu, wd):
    rows, d = x_loc.shape
    ff = wg.shape[2]
    n_tiles = tile_expert.shape[0]
    nj = ff // MOE_FF_TILE
    hbm = pl.BlockSpec(memory_space=pl.ANY)
    return pl.pallas_call(
        _experts_kernel,
        grid_spec=pltpu.PrefetchScalarGridSpec(
            num_scalar_prefetch=3,
            grid=(n_tiles,),
            in_specs=[hbm, hbm, hbm, hbm],
            out_specs=hbm,
            scratch_shapes=[pltpu.VMEM((2, FFN_ROW_TILE, d), BF16),
                            pltpu.VMEM((FFN_ROW_TILE, d), BF16),
                            pltpu.VMEM((FFN_ROW_TILE, ff), BF16),
                            pltpu.VMEM((nj, d, MOE_FF_TILE), BF16),
                            pltpu.VMEM((nj, d, MOE_FF_TILE), BF16),
                            pltpu.VMEM((ff, d), BF16),
                            pltpu.SemaphoreType.DMA((2,)),
                            pltpu.SemaphoreType.DMA((1,)),
                            pltpu.SemaphoreType.DMA((nj + 1,))]),
        out_shape=jax.ShapeDtypeStruct((rows, d), BF16),
        input_output_aliases={3: 0},
        compiler_params=_params("arbitrary"),
        name="swiglu_experts",
    )(tile_expert, used, gran, x_loc, wg, wu, wd)


def _combine_ln_kernel(y_ref, row_ref, w_ref, x_ref, g_ref, b_ref, o_ref):
    tb, rows = x_ref.shape[0], y_ref.shape[0]
    c = lax.broadcasted_iota(jnp.int32, (tb, rows), 1)
    q = (jnp.where(c == row_ref[:, 0:1], w_ref[:, 0:1], 0.0)
         + jnp.where(c == row_ref[:, 1:2], w_ref[:, 1:2], 0.0))
    f = jnp.dot(q.astype(BF16), y_ref[...], preferred_element_type=F32)
    o_ref[...] = _layer_norm(DEEPNORM_ALPHA * x_ref[...] + f, g_ref[...], b_ref[...])


def _combine_ln(y_loc, row_col, w_col, x, g, b):
    t, d = x.shape
    return pl.pallas_call(
        _combine_ln_kernel,
        grid=(t // MOE_BLOCK,),
        in_specs=[pl.BlockSpec((LOC_ROWS, d), lambda i: (i, 0)),
                  pl.BlockSpec((MOE_BLOCK, TOP_K), lambda i: (i, 0)),
                  pl.BlockSpec((MOE_BLOCK, TOP_K), lambda i: (i, 0)),
                  pl.BlockSpec((MOE_BLOCK, d), lambda i: (i, 0)),
                  pl.BlockSpec((1, d), lambda i: (0, 0)),
                  pl.BlockSpec((1, d), lambda i: (0, 0))],
        out_specs=pl.BlockSpec((MOE_BLOCK, d), lambda i: (i, 0)),
        out_shape=jax.ShapeDtypeStruct((t, d), F32),
        compiler_params=_params("parallel"),
        name="combine_ln",
    )(y_loc, row_col, w_col, x, g, b)


def _mixer(x, batch, seq, w_in, gm_ws, gm_bs, gm_ln_g, gm_ln_b, gla_wa2, gla_ba, gla_norm_g,
           w_out, ln_g, ln_b):
    w_gate = _fold_gate(w_in[:, PROJ_MAIN:], gla_wa2)
    w_ext = jnp.concatenate([w_in[:, :PROJ_MAIN], w_gate], axis=1).astype(BF16)
    del batch
    return _mixer_call(x, w_ext, w_out.astype(BF16), ln_g.reshape(1, -1), ln_b.reshape(1, -1),
                       gm_ws, gm_bs.reshape(GM_HEADS, GM_CHUNK, 1),
                       gm_ln_g.reshape(1, -1), gm_ln_b.reshape(1, -1),
                       gla_ba.reshape(1, -1), gla_norm_g.reshape(1, -1), seq)


def _granule_table(counts):
    nb = counts.shape[0]
    n_tiles = (nb * (TOP_K * MOE_BLOCK + N_EXPERTS * (GRANULE - 1))) // FFN_ROW_TILE + N_EXPERTS
    gran_per_block = LOC_ROWS // GRANULE
    g = (counts + GRANULE - 1) // GRANULE
    seg_start = jnp.cumsum(g, axis=1) - g
    cum_incl = jnp.cumsum(g, axis=0)
    cum_excl = cum_incl - g
    total = cum_incl[-1]
    tiles_per_expert = (total + GRAN_PER_TILE - 1) // GRAN_PER_TILE
    tile_end = jnp.cumsum(tiles_per_expert)
    n_used = tile_end[-1:]
    tile = jnp.arange(n_tiles, dtype=jnp.int32)
    tile_expert = jnp.minimum(jnp.sum(tile[:, None] >= tile_end[None, :], axis=1), N_EXPERTS - 1)
    pick = (tile_expert[:, None] == jnp.arange(N_EXPERTS, dtype=jnp.int32)[None, :]).astype(jnp.int32)
    per_tile = lambda tab: jnp.sum(pick[:, :, None] * tab.T[None, :, :], axis=1)
    first_tile = jnp.sum(pick * (tile_end - tiles_per_expert)[None, :], axis=1)
    vg = ((tile - first_tile)[:, None] * GRAN_PER_TILE
          + jnp.arange(GRAN_PER_TILE, dtype=jnp.int32)[None, :])[:, :, None]
    lo = per_tile(cum_excl)[:, None, :]
    hi = per_tile(cum_incl)[:, None, :]
    base = (jnp.arange(nb, dtype=jnp.int32)[None, :] * gran_per_block + per_tile(seg_start))[:, None, :]
    inside = jnp.logical_and(vg >= lo, vg < hi)
    phys = jnp.sum(jnp.where(inside, base + vg - lo, 0), axis=2)
    valid = jnp.logical_and(jnp.any(inside, axis=2), (tile < n_used[0])[:, None])
    gran = jnp.where(valid, phys, 0).astype(jnp.int32).reshape(-1)
    used = jnp.concatenate([n_used, jnp.sum(valid, axis=1)]).astype(jnp.int32)
    return tile_expert.astype(jnp.int32), used, gran


def _moe_ffn(x, w_router, e_gate, e_up, e_down, ln_g, ln_b):
    ri, rw, cnt, x_loc = _route_dispatch(x, w_router.T)
    tile_expert, n_used, gran = _granule_table(cnt[:, :, 0].astype(jnp.int32))
    y_loc = _experts(tile_expert, n_used, gran, x_loc,
                     e_gate.astype(BF16), e_up.astype(BF16), e_down.astype(BF16))
    return _combine_ln(y_loc, ri[2:2 + TOP_K].T, rw[:TOP_K].T, x,
                       ln_g.reshape(1, -1), ln_b.reshape(1, -1))


def kernel(x, w_in, gm_ws, gm_bs, gm_ln_g, gm_ln_b, gla_wa2, gla_ba, gla_norm_g, w_out, ln_mix_g, ln_mix_b, ffn_w_gate, ffn_w_up, ffn_w_down, router_w, exp_w_gate, exp_w_up, exp_w_down, ln_ffn_g, ln_ffn_b):
    batch, seq, d = x.shape
    h = x.reshape(batch * seq, d)
    for layer in range(DEPTH):
        h = _mixer(h, batch, seq, w_in[layer], gm_ws[layer], gm_bs[layer], gm_ln_g[layer],
                   gm_ln_b[layer], gla_wa2[layer], gla_ba[layer], gla_norm_g[layer], w_out[layer],
                   ln_mix_g[layer], ln_mix_b[layer])
        i = layer // 2
        if layer % 2 == 0:
            h = _swiglu_ln(h, ffn_w_gate[i].astype(BF16), ffn_w_up[i].astype(BF16),
                           ffn_w_down[i].astype(BF16), ln_ffn_g[layer].reshape(1, -1),
                           ln_ffn_b[layer].reshape(1, -1))
        else:
            h = _moe_ffn(h, router_w[i], exp_w_gate[i], exp_w_up[i], exp_w_down[i],
                         ln_ffn_g[layer], ln_ffn_b[layer])
    return h.reshape(batch, seq, d)
```

```python
import functools

import jax
import jax.numpy as jnp
from jax import lax
from jax.experimental import pallas as pl
from jax.experimental.pallas import tpu as pltpu

F32 = jnp.float32
BF16 = jnp.bfloat16

D_MODEL = 1024
DEPTH = 2
GM_HEADS = 4
GM_WIDTH = 512
GM_HEAD_DIM = 128
GM_CHUNK = 128
GLA_HEADS = 4
GLA_WIDTH = 512
GLA_DV = 128
GLA_KEY_WIDTH = 256
GLA_DK = 64
GLA_GATE_RANK = 16
GLA_TAU = 16.0
GLA_CHUNK = 64
PROJ_MAIN = 2 * GM_WIDTH + 2 * GLA_KEY_WIDTH + 2 * GLA_WIDTH
N_EXPERTS = 8
TOP_K = 2
LN_EPS = 1e-5
RMS_EPS = 1e-6
DEEPNORM_ALPHA = (2 * DEPTH) ** 0.25

V7X_VMEM_LIMIT_BYTES = 56 * 1024 * 1024
LANE = 128
SUBLANE = 8
MXU_N = 256

ROW_TILE = 512
FFN_ROW_TILE = 1024
DENSE_FF_TILE = 256
MOE_FF_TILE = 512
MOE_BLOCK = 512
GRANULE = 16
LOC_ROWS = -(-(TOP_K * MOE_BLOCK + N_EXPERTS * (GRANULE - 1)) // LANE) * LANE
COMBINE_SUB = 256
MIXER_HEAD_FILLERS = 6
MIXER_TAIL_FILLERS = 3
GRAN_UNROLL = 4
GRAN_PER_TILE = FFN_ROW_TILE // GRANULE


def _params(*sem, **kw):
    return pltpu.CompilerParams(dimension_semantics=sem, vmem_limit_bytes=V7X_VMEM_LIMIT_BYTES, **kw)


def _gelu(x):
    return 0.5 * x * (1.0 + lax.erf(x * (2.0 ** -0.5)))


def _layer_norm(y, g, b):
    mu = jnp.mean(y, axis=-1, keepdims=True)
    d = y - mu
    var = jnp.mean(d * d, axis=-1, keepdims=True)
    return d * lax.rsqrt(var + LN_EPS) * g + b


def _fold_kernel(w1_ref, w2_ref, o_ref):
    acc = jnp.zeros(o_ref.shape, F32)
    for r in range(GLA_GATE_RANK):
        acc = acc + w1_ref[:, r:r + 1] * w2_ref[r:r + 1, :]
    o_ref[...] = acc


def _fold_gate(w1, w2):
    return pl.pallas_call(
        _fold_kernel,
        out_shape=jax.ShapeDtypeStruct((w1.shape[0], w2.shape[1]), F32),
        name="fold_gate",
    )(w1, w2)


def _inproj_items(x_ref, w_ref, p_ref, z_ref):
    cache = {}

    def xb():
        if "xb" not in cache:
            cache["xb"] = x_ref[...].astype(BF16)
        return cache["xb"]

    def block(n):
        cols = slice(n * MXU_N, (n + 1) * MXU_N)

        def run():
            p_ref[:, cols] = jnp.dot(xb(), w_ref[:, cols], preferred_element_type=F32).astype(BF16)
        return run

    def gate_logits():
        z_ref[...] = jnp.dot(xb(), w_ref[:, PROJ_MAIN:], preferred_element_type=F32)

    return [block(n) for n in range(PROJ_MAIN // MXU_N)] + [gate_logits]


def _gmlp_items(uv_ref, w_ref, bcol_ref, g_ref, b_ref, o_ref):
    rows = uv_ref.shape[0]

    def head(h):
        cols = slice(h * GM_HEAD_DIM, (h + 1) * GM_HEAD_DIM)
        vcols = slice(GM_WIDTH + h * GM_HEAD_DIM, GM_WIDTH + (h + 1) * GM_HEAD_DIM)

        def run():
            r = lax.broadcasted_iota(jnp.int32, (GM_CHUNK, GM_CHUNK), 0)
            c = lax.broadcasted_iota(jnp.int32, (GM_CHUNK, GM_CHUNK), 1)
            wm = jnp.where(r >= c, w_ref[h], 0.0).astype(BF16)
            v = _gelu(uv_ref[:, vcols].astype(F32))
            vn = _layer_norm(v, g_ref[:, cols], b_ref[:, cols]).astype(BF16)
            u = _gelu(uv_ref[:, cols].astype(F32))
            for ci in range(rows // GM_CHUNK):
                rs = slice(ci * GM_CHUNK, (ci + 1) * GM_CHUNK)
                s = jnp.dot(wm, vn[rs], preferred_element_type=F32) + bcol_ref[h]
                o_ref[rs, cols] = (u[rs] * s).astype(BF16)
        return run

    return [head(h) for h in range(GM_HEADS)]


def _gla_items(q_ref, k_ref, v_ref, g_ref, z_ref, ba_ref, ng_ref, o_ref,
               st_ref, qd_ref, ki_ref, kd_ref, dec_ref, raw_ref):
    rows = q_ref.shape[0]
    n_chunks = rows // GLA_CHUNK
    shift = GLA_CHUNK.bit_length() - 1
    nt = (((1,), (1,)), ((), ()))
    tn = (((0,), (0,)), ((), ()))

    def prep():
        r = lax.broadcasted_iota(jnp.int32, (rows, rows), 0)
        c = lax.broadcasted_iota(jnp.int32, (rows, rows), 1)
        ones_tril = jnp.where(jnp.logical_and((r >> shift) == (c >> shift), r >= c),
                              1.0, 0.0).astype(BF16)
        z = z_ref[...] + ba_ref[...]
        la = (jnp.minimum(z, 0.0) - jnp.log1p(jnp.exp(-jnp.abs(z)))) * (1.0 / GLA_TAU)
        la_hi = la.astype(BF16)
        la_lo = (la - la_hi.astype(F32)).astype(BF16)
        b = (jnp.dot(ones_tril, la_hi, preferred_element_type=F32)
             + jnp.dot(ones_tril, la_lo, preferred_element_type=F32))
        b3 = b.reshape(n_chunks, GLA_CHUNK, GLA_KEY_WIDTH)
        b_last = b3[:, GLA_CHUNK - 1:GLA_CHUNK, :]
        k = k_ref[...].astype(F32)
        qd_ref[...] = (q_ref[...].astype(F32) * (GLA_DK ** -0.5) * jnp.exp(b)).astype(BF16)
        ki_ref[...] = (k * jnp.exp(-b)).astype(BF16)
        kd_ref[...] = (k * jnp.exp(b_last - b3).reshape(rows, GLA_KEY_WIDTH)).astype(BF16)
        dec_ref[...] = jnp.exp(b_last).reshape(n_chunks, GLA_KEY_WIDTH)

    def chunk(ci):
        rs = slice(ci * GLA_CHUNK, (ci + 1) * GLA_CHUNK)

        def run(between=None):
            rc = lax.broadcasted_iota(jnp.int32, (GLA_CHUNK, GLA_CHUNK), 0)
            cc = lax.broadcasted_iota(jnp.int32, (GLA_CHUNK, GLA_CHUNK), 1)
            causal = rc >= cc
            q_dec = qd_ref[rs, :]
            k_inv = ki_ref[rs, :]
            k_dec = kd_ref[rs, :]
            decay = dec_ref[ci:ci + 1, :]
            head_cols = [slice(h * GLA_DK, (h + 1) * GLA_DK) for h in range(GLA_HEADS)]
            all_scores = [lax.dot_general(q_dec[:, ks], k_inv[:, ks], nt, preferred_element_type=F32)
                          for ks in head_cols]
            if between is not None:
                between()
            for h in range(GLA_HEADS):
                ks = head_cols[h]
                vs = slice(h * GLA_DV, (h + 1) * GLA_DV)
                scores = jnp.where(causal, all_scores[h], 0.0).astype(BF16)
                vh = v_ref[rs, vs]
                state_t = st_ref[h]
                o = (jnp.dot(scores, vh, preferred_element_type=F32)
                     + lax.dot_general(q_dec[:, ks], state_t.astype(BF16), nt,
                                       preferred_element_type=F32))
                st_ref[h] = (state_t * decay[:, ks]
                             + lax.dot_general(vh, k_dec[:, ks], tn, preferred_element_type=F32))
                raw_ref[rs, vs] = o
        return run

    def finish():
        for h in range(GLA_HEADS):
            vs = slice(h * GLA_DV, (h + 1) * GLA_DV)
            o = raw_ref[:, vs]
            o = o * lax.rsqrt(jnp.mean(o * o, axis=-1, keepdims=True) + RMS_EPS) * ng_ref[:, vs]
            gate = g_ref[:, vs].astype(F32)
            o_ref[:, vs] = (o * (gate * jax.nn.sigmoid(gate))).astype(BF16)

    return prep, [chunk(ci) for ci in range(n_chunks)], finish


def _mixer_kernel(xa_ref, xc_ref, win_ref, wout_ref, lng_ref, lnb_ref,
                  gw_ref, gbcol_ref, gg_ref, gb_ref, ba_ref, ng_ref, o_ref,
                  proj0, proj1, z0, z1, ab0, ab1, st_ref, *gla_scratch, tiles_per_seq):
    t = pl.program_id(0)
    proj_scr, z_scr, ab_scr = (proj0, proj1), (z0, z1), (ab0, ab1)

    @pl.when(t == 0)
    def _():
        for ref in (proj0, proj1, z0, z1, ab0, ab1, st_ref):
            ref[...] = jnp.zeros(ref.shape, ref.dtype)

    @pl.when(lax.rem(t + tiles_per_seq - 1, tiles_per_seq) == 0)
    def _():
        st_ref[...] = jnp.zeros(st_ref.shape, F32)

    q0 = 2 * GM_WIDTH
    k0 = q0 + GLA_KEY_WIDTH
    v0 = k0 + GLA_KEY_WIDTH
    g0 = v0 + GLA_WIDTH

    def stages(proj_a, z_a, proj_b, z_b, ab_b, ab_c):
        inproj = _inproj_items(xa_ref, win_ref, proj_a, z_a)
        gla_prep, gla_chunks, gla_finish = _gla_items(
            proj_b.at[:, q0:k0], proj_b.at[:, k0:v0], proj_b.at[:, v0:g0],
            proj_b.at[:, g0:PROJ_MAIN], z_b, ba_ref, ng_ref, ab_b.at[:, GM_WIDTH:],
            st_ref, *gla_scratch)
        gmlp = _gmlp_items(proj_b.at[:, :q0], gw_ref, gbcol_ref, gg_ref, gb_ref,
                           ab_b.at[:, :GM_WIDTH])
        outproj, outproj_finish = _outproj_items(ab_c.at[:, :GM_WIDTH], ab_c.at[:, GM_WIDTH:],
                                                 wout_ref, xc_ref, lng_ref, lnb_ref, o_ref)
        fillers = inproj + outproj
        head, tail = MIXER_HEAD_FILLERS, len(fillers) - MIXER_TAIL_FILLERS
        for run in fillers[:head // 2]:
            run()
        gla_prep()
        for run in fillers[head // 2:head]:
            run()
        done = head
        for i, run_chunk in enumerate(gla_chunks):
            target = head + (i + 1) * (tail - head) // len(gla_chunks)
            mid, done = fillers[done:target], target
            run_chunk(between=lambda mid=mid: [run() for run in mid])
        late = fillers[tail:] + [gla_finish]
        for i, run_head in enumerate(gmlp):
            for run in late[i * len(late) // len(gmlp):(i + 1) * len(late) // len(gmlp)]:
                run()
            run_head()
        outproj_finish()

    for parity in range(2):
        @pl.when(t % 2 == parity)
        def _(a=parity, b=1 - parity):
            stages(proj_scr[a], z_scr[a], proj_scr[b], z_scr[b], ab_scr[b], ab_scr[a])


def _mixer_call(x, w_ext, w_out, ln_g, ln_b, gm_ws, gm_bcol, gm_ln_g, gm_ln_b, ba, norm_g, seq):
    t, d = x.shape
    n_tiles = t // ROW_TILE
    kw, vw = GLA_KEY_WIDTH, GLA_WIDTH
    const2 = lambda i: (0, 0)
    const3 = lambda i: (0, 0, 0)
    resident = pl.Buffered(1)
    tile_c = lambda i: (jnp.maximum(i - 2, 0), 0)
    return pl.pallas_call(
        functools.partial(_mixer_kernel, tiles_per_seq=seq // ROW_TILE),
        grid=(n_tiles + 2,),
        in_specs=[pl.BlockSpec((ROW_TILE, d), lambda i: (jnp.minimum(i, n_tiles - 1), 0)),
                  pl.BlockSpec((ROW_TILE, d), tile_c),
                  pl.BlockSpec(w_ext.shape, const2, pipeline_mode=resident),
                  pl.BlockSpec(w_out.shape, const2, pipeline_mode=resident),
                  pl.BlockSpec((1, d), const2),
                  pl.BlockSpec((1, d), const2),
                  pl.BlockSpec((GM_HEADS, GM_CHUNK, GM_CHUNK), const3),
                  pl.BlockSpec((GM_HEADS, GM_CHUNK, 1), const3),
                  pl.BlockSpec((1, GM_WIDTH), const2),
                  pl.BlockSpec((1, GM_WIDTH), const2),
                  pl.BlockSpec((1, kw), const2),
                  pl.BlockSpec((1, vw), const2)],
        out_specs=pl.BlockSpec((ROW_TILE, d), tile_c),
        out_shape=jax.ShapeDtypeStruct((t, d), F32),
        scratch_shapes=[pltpu.VMEM((ROW_TILE, PROJ_MAIN), BF16),
                        pltpu.VMEM((ROW_TILE, PROJ_MAIN), BF16),
                        pltpu.VMEM((ROW_TILE, kw), F32),
                        pltpu.VMEM((ROW_TILE, kw), F32),
                        pltpu.VMEM((ROW_TILE, GM_WIDTH + vw), BF16),
                        pltpu.VMEM((ROW_TILE, GM_WIDTH + vw), BF16),
                        pltpu.VMEM((GLA_HEADS, GLA_DV, GLA_DK), F32),
                        pltpu.VMEM((ROW_TILE, kw), BF16),
                        pltpu.VMEM((ROW_TILE, kw), BF16),
                        pltpu.VMEM((ROW_TILE, kw), BF16),
                        pltpu.VMEM((ROW_TILE // GLA_CHUNK, kw), F32),
                        pltpu.VMEM((ROW_TILE, vw), F32)],
        compiler_params=_params("arbitrary"),
        name="mixer",
    )(x, x, w_ext, w_out, ln_g, ln_b, gm_ws, gm_bcol, gm_ln_g, gm_ln_b, ba, norm_g)


def _outproj_items(a_ref, b_ref, w_ref, x_ref, g_ref, be_ref, o_ref):
    def block(n):
        cols = slice(n * MXU_N, (n + 1) * MXU_N)

        def run():
            h = (jnp.dot(a_ref[...], w_ref[:GM_WIDTH, cols], preferred_element_type=F32)
                 + jnp.dot(b_ref[...], w_ref[GM_WIDTH:, cols], preferred_element_type=F32))
            o_ref[:, cols] = DEEPNORM_ALPHA * x_ref[:, cols] + h
        return run

    def finish():
        o_ref[...] = _layer_norm(o_ref[...], g_ref[...], be_ref[...])

    return [block(n) for n in range(o_ref.shape[1] // MXU_N)], finish


def _swiglu_ln_kernel(x_ref, wg_ref, wu_ref, wd_ref, g_ref, b_ref, o_ref, hid_ref):
    xb = x_ref[...].astype(BF16)
    for n in range(wg_ref.shape[1] // DENSE_FF_TILE):
        cols = slice(n * DENSE_FF_TILE, (n + 1) * DENSE_FF_TILE)
        gate = jnp.dot(xb, wg_ref[:, cols], preferred_element_type=F32)
        up = jnp.dot(xb, wu_ref[:, cols], preferred_element_type=F32)
        hid_ref[:, cols] = (gate * jax.nn.sigmoid(gate) * up).astype(BF16)
    f = jnp.dot(hid_ref[...], wd_ref[...], preferred_element_type=F32)
    o_ref[...] = _layer_norm(DEEPNORM_ALPHA * x_ref[...] + f, g_ref[...], b_ref[...])


def _swiglu_ln(x, wg, wu, wd, g, b):
    rows, d = x.shape
    ff = wg.shape[1]
    resident = pl.Buffered(1)
    return pl.pallas_call(
        _swiglu_ln_kernel,
        grid=(rows // FFN_ROW_TILE,),
        in_specs=[pl.BlockSpec((FFN_ROW_TILE, d), lambda i: (i, 0)),
                  pl.BlockSpec((d, ff), lambda i: (0, 0), pipeline_mode=resident),
                  pl.BlockSpec((d, ff), lambda i: (0, 0), pipeline_mode=resident),
                  pl.BlockSpec((ff, d), lambda i: (0, 0), pipeline_mode=resident),
                  pl.BlockSpec((1, d), lambda i: (0, 0)),
                  pl.BlockSpec((1, d), lambda i: (0, 0))],
        out_specs=pl.BlockSpec((FFN_ROW_TILE, d), lambda i: (i, 0)),
        out_shape=jax.ShapeDtypeStruct((rows, d), F32),
        scratch_shapes=[pltpu.VMEM((FFN_ROW_TILE, ff), BF16)],
        compiler_params=_params("parallel"),
        name="swiglu_ln",
    )(x, wg, wu, wd, g, b)


def _split_bf16(v):
    hi = v.astype(BF16)
    return hi, (v - hi.astype(F32)).astype(BF16)


def _router_kernel(x_ref, wr_ref, ri_ref, rw_ref, cnt_ref):
    tb = x_ref.shape[0]
    x_hi, x_lo = _split_bf16(x_ref[...])
    w_hi, w_lo = _split_bf16(wr_ref[...])
    nt = (((1,), (1,)), ((), ()))
    logits = (lax.dot_general(w_hi, x_hi, nt, preferred_element_type=F32)
              + lax.dot_general(w_hi, x_lo, nt, preferred_element_type=F32)
              + lax.dot_general(w_lo, x_hi, nt, preferred_element_type=F32))
    e_iota = lax.broadcasted_iota(jnp.int32, (N_EXPERTS, tb), 0)
    m0 = jnp.max(logits, axis=0, keepdims=True)
    i0 = jnp.min(jnp.where(logits == m0, e_iota, N_EXPERTS), axis=0, keepdims=True)
    rest = jnp.where(e_iota == i0, -jnp.inf, logits)
    m1 = jnp.max(rest, axis=0, keepdims=True)
    i1 = jnp.min(jnp.where(rest == m1, e_iota, N_EXPERTS), axis=0, keepdims=True)
    ex = jnp.exp(m1 - m0)
    w0 = 1.0 / (1.0 + ex)
    w1 = ex / (1.0 + ex)
    oh0 = e_iota == i0
    oh1 = e_iota == i1
    onehot = jnp.where(jnp.logical_or(oh0, oh1), 1.0, 0.0)
    tr = lax.broadcasted_iota(jnp.int32, (tb, tb), 0)
    tc = lax.broadcasted_iota(jnp.int32, (tb, tb), 1)
    before = jnp.where(tr < tc, 1.0, 0.0).astype(BF16)
    rank = jnp.dot(onehot.astype(BF16), before, preferred_element_type=F32)
    count = jnp.broadcast_to(jnp.sum(onehot, axis=1, keepdims=True), (N_EXPERTS, LANE))
    cap = jnp.ceil(count * (1.0 / GRANULE)) * GRANULE
    e_sub = lax.broadcasted_iota(jnp.int32, (N_EXPERTS, LANE), 0)
    seg = jnp.zeros((N_EXPERTS, LANE), F32)
    for e in range(N_EXPERTS - 1):
        seg = seg + jnp.where(e_sub > e, cap[e:e + 1, :], 0.0)
    row = rank + seg[:, 0:1]
    row0 = jnp.sum(jnp.where(oh0, row, 0.0), axis=0, keepdims=True)
    row1 = jnp.sum(jnp.where(oh1, row, 0.0), axis=0, keepdims=True)
    cnt_ref[...] = count
    ri_ref[...] = jnp.zeros(ri_ref.shape, jnp.int32)
    ri_ref[0:1, :] = i0
    ri_ref[1:2, :] = i1
    ri_ref[2:3, :] = row0.astype(jnp.int32)
    ri_ref[3:4, :] = row1.astype(jnp.int32)
    rw_ref[...] = jnp.zeros(rw_ref.shape, F32)
    rw_ref[0:1, :] = w0
    rw_ref[1:2, :] = w1
    return x_hi


def _route_dispatch_kernel(x_ref, wr_ref, ri_ref, rw_ref, cnt_ref, o_ref):
    xb = _router_kernel(x_ref, wr_ref, ri_ref, rw_ref, cnt_ref)
    rows, tb = o_ref.shape[0], x_ref.shape[0]
    r = lax.broadcasted_iota(jnp.int32, (rows, tb), 0)
    sel = jnp.logical_or(r == ri_ref[2:3, :], r == ri_ref[3:4, :])
    perm = jnp.where(sel, 1.0, 0.0).astype(BF16)
    o_ref[...] = jnp.dot(perm, xb, preferred_element_type=F32).astype(BF16)


def _route_dispatch(x, wr_t):
    t, d = x.shape
    nb = t // MOE_BLOCK
    return pl.pallas_call(
        _route_dispatch_kernel,
        grid=(nb,),
        in_specs=[pl.BlockSpec((MOE_BLOCK, d), lambda i: (i, 0)),
                  pl.BlockSpec((N_EXPERTS, d), lambda i: (0, 0))],
        out_specs=[pl.BlockSpec((SUBLANE, MOE_BLOCK), lambda i: (0, i)),
                   pl.BlockSpec((SUBLANE, MOE_BLOCK), lambda i: (0, i)),
                   pl.BlockSpec((None, N_EXPERTS, LANE), lambda i: (i, 0, 0)),
                   pl.BlockSpec((LOC_ROWS, d), lambda i: (i, 0))],
        out_shape=[jax.ShapeDtypeStruct((SUBLANE, t), jnp.int32),
                   jax.ShapeDtypeStruct((SUBLANE, t), F32),
                   jax.ShapeDtypeStruct((nb, N_EXPERTS, LANE), F32),
                   jax.ShapeDtypeStruct((nb * LOC_ROWS, d), BF16)],
        compiler_params=_params("parallel"),
        name="route_dispatch",
    )(x, wr_t)


def _experts_kernel(te_ref, nu_ref, gran_ref, x_hbm, wg_hbm, wu_hbm, wd_hbm, y_hbm,
                    xbuf, ybuf, hid_ref, wg_buf, wu_buf, wd_buf, sem_in, sem_out, sem_w):
    i = pl.program_id(0)
    n_tiles = pl.num_programs(0)
    n_used = nu_ref[0]
    n_chunks = wg_buf.shape[0]
    expert = te_ref[i]
    prev_expert = te_ref[jnp.maximum(i - 1, 0)]
    next_expert = te_ref[jnp.minimum(i + 1, n_tiles - 1)]
    first_of_expert = jnp.logical_or(i == 0, prev_expert != expert)
    refill = jnp.logical_and(i + 1 < n_used, next_expert != expert)

    def up_copies(e, n):
        cols = pl.ds(n * MOE_FF_TILE, MOE_FF_TILE)
        return (pltpu.make_async_copy(wg_hbm.at[e, :, cols], wg_buf.at[n], sem_w.at[n]),
                pltpu.make_async_copy(wu_hbm.at[e, :, cols], wu_buf.at[n], sem_w.at[n]))

    def down_copy(e):
        return pltpu.make_async_copy(wd_hbm.at[e], wd_buf, sem_w.at[n_chunks])

    def for_granules(tile, fn):
        def one(s):
            g = gran_ref[tile * GRAN_PER_TILE + s]
            fn(pl.ds(pl.multiple_of(g * GRANULE, GRANULE), GRANULE),
               pl.ds(pl.multiple_of(s * GRANULE, GRANULE), GRANULE))

        def group(p, carry):
            for u in range(GRAN_UNROLL):
                one(p * GRAN_UNROLL + u)
            return carry

        def single(s, carry):
            one(s)
            return carry

        filled = nu_ref[1 + tile]
        n_groups = filled // GRAN_UNROLL
        lax.fori_loop(0, n_groups, group, 0)
        lax.fori_loop(n_groups * GRAN_UNROLL, filled, single, 0)

    def gather(slot):
        return lambda hbm_rows, tile_rows: pltpu.make_async_copy(
            x_hbm.at[hbm_rows], xbuf.at[slot, tile_rows], sem_in.at[slot])

    def scatter(hbm_rows, tile_rows):
        return pltpu.make_async_copy(ybuf.at[tile_rows], y_hbm.at[hbm_rows], sem_out.at[0])

    @pl.when(i < n_used)
    def _():
        slot = i % 2

        @pl.when(i == 0)
        def _():
            for n in range(n_chunks):
                for cp in up_copies(expert, n):
                    cp.start()
            down_copy(expert).start()
            xbuf[...] = jnp.zeros(xbuf.shape, BF16)
            for_granules(0, lambda h, r: gather(0)(h, r).start())

        for_granules(i, lambda h, r: gather(slot)(h, r).wait())

        @pl.when(i + 1 < n_used)
        def _():
            for_granules(i + 1, lambda h, r: gather(1 - slot)(h, r).start())

        xb = xbuf[slot]
        for n in range(n_chunks):
            @pl.when(first_of_expert)
            def _():
                for cp in up_copies(expert, n):
                    cp.wait()

            gate = jnp.dot(xb, wg_buf[n], preferred_element_type=F32)
            up = jnp.dot(xb, wu_buf[n], preferred_element_type=F32)
            hid_ref[:, n * MOE_FF_TILE:(n + 1) * MOE_FF_TILE] = (
                gate * jax.nn.sigmoid(gate) * up).astype(BF16)

            @pl.when(refill)
            def _():
                for cp in up_copies(next_expert, n):
                    cp.start()

        @pl.when(first_of_expert)
        def _():
            down_copy(expert).wait()

        @pl.when(i > 0)
        def _():
            for_granules(i - 1, lambda h, r: scatter(h, r).wait())

        ybuf[...] = jnp.dot(hid_ref[...], wd_buf[...], preferred_element_type=F32).astype(BF16)

        @pl.when(refill)
        def _():
            down_copy(next_expert).start()

        for_granules(i, lambda h, r: scatter(h, r).start())

        @pl.when(i == n_used - 1)
        def _():
            for_granules(i, lambda h, r: scatter(h, r).wait())


def _experts(tile_expert, used, gran, x_loc, wg, wu, wd):
    rows, d = x_loc.shape
    ff = wg.shape[2]
    n_tiles = tile_expert.shape[0]
    nj = ff // MOE_FF_TILE
    hbm = pl.BlockSpec(memory_space=pl.ANY)
    return pl.pallas_call(
        _experts_kernel,
        grid_spec=pltpu.PrefetchScalarGridSpec(
            num_scalar_prefetch=3,
            grid=(n_tiles,),
            in_specs=[hbm, hbm, hbm, hbm],
            out_specs=hbm,
            scratch_shapes=[pltpu.VMEM((2, FFN_ROW_TILE, d), BF16),
                            pltpu.VMEM((FFN_ROW_TILE, d), BF16),
                            pltpu.VMEM((FFN_ROW_TILE, ff), BF16),
                            pltpu.VMEM((nj, d, MOE_FF_TILE), BF16),
                            pltpu.VMEM((nj, d, MOE_FF_TILE), BF16),
                            pltpu.VMEM((ff, d), BF16),
                            pltpu.SemaphoreType.DMA((2,)),
                            pltpu.SemaphoreType.DMA((1,)),
                            pltpu.SemaphoreType.DMA((nj + 1,))]),
        out_shape=jax.ShapeDtypeStruct((rows, d), BF16),
        input_output_aliases={3: 0},
        compiler_params=_params("arbitrary"),
        name="swiglu_experts",
    )(tile_expert, used, gran, x_loc, wg, wu, wd)


def _combine_ln_kernel(y_ref, row_ref, w_ref, x_ref, g_ref, b_ref, o_ref):
    tb, rows = x_ref.shape[0], y_ref.shape[0]
    for k in range(tb // COMBINE_SUB):
        ts = slice(k * COMBINE_SUB, (k + 1) * COMBINE_SUB)
        c = lax.broadcasted_iota(jnp.int32, (COMBINE_SUB, rows), 1)
        q = (jnp.where(c == row_ref[ts, 0:1], w_ref[ts, 0:1], 0.0)
             + jnp.where(c == row_ref[ts, 1:2], w_ref[ts, 1:2], 0.0))
        f = jnp.dot(q.astype(BF16), y_ref[...], preferred_element_type=F32)
        o_ref[ts, :] = _layer_norm(DEEPNORM_ALPHA * x_ref[ts, :] + f, g_ref[...], b_ref[...])


def _combine_ln(y_loc, row_col, w_col, x, g, b):
    t, d = x.shape
    return pl.pallas_call(
        _combine_ln_kernel,
        grid=(t // MOE_BLOCK,),
        in_specs=[pl.BlockSpec((LOC_ROWS, d), lambda i: (i, 0)),
                  pl.BlockSpec((MOE_BLOCK, TOP_K), lambda i: (i, 0)),
                  pl.BlockSpec((MOE_BLOCK, TOP_K), lambda i: (i, 0)),
                  pl.BlockSpec((MOE_BLOCK, d), lambda i: (i, 0)),
                  pl.BlockSpec((1, d), lambda i: (0, 0)),
                  pl.BlockSpec((1, d), lambda i: (0, 0))],
        out_specs=pl.BlockSpec((MOE_BLOCK, d), lambda i: (i, 0)),
        out_shape=jax.ShapeDtypeStruct((t, d), F32),
        compiler_params=_params("parallel"),
        name="combine_ln",
    )(y_loc, row_col, w_col, x, g, b)


def _mixer(x, batch, seq, w_in, gm_ws, gm_bs, gm_ln_g, gm_ln_b, gla_wa2, gla_ba, gla_norm_g,
           w_out, ln_g, ln_b):
    w_gate = _fold_gate(w_in[:, PROJ_MAIN:], gla_wa2)
    w_ext = jnp.concatenate([w_in[:, :PROJ_MAIN], w_gate], axis=1).astype(BF16)
    del batch
    return _mixer_call(x, w_ext, w_out.astype(BF16), ln_g.reshape(1, -1), ln_b.reshape(1, -1),
                       gm_ws, gm_bs.reshape(GM_HEADS, GM_CHUNK, 1),
                       gm_ln_g.reshape(1, -1), gm_ln_b.reshape(1, -1),
                       gla_ba.reshape(1, -1), gla_norm_g.reshape(1, -1), seq)


def _granule_table(counts):
    nb = counts.shape[0]
    n_tiles = (nb * (TOP_K * MOE_BLOCK + N_EXPERTS * (GRANULE - 1))) // FFN_ROW_TILE + N_EXPERTS
    gran_per_block = LOC_ROWS // GRANULE
    g = (counts + GRANULE - 1) // GRANULE
    seg_start = jnp.cumsum(g, axis=1) - g
    cum_incl = jnp.cumsum(g, axis=0)
    cum_excl = cum_incl - g
    total = cum_incl[-1]
    tiles_per_expert = (total + GRAN_PER_TILE - 1) // GRAN_PER_TILE
    tile_end = jnp.cumsum(tiles_per_expert)
    n_used = tile_end[-1:]
    tile = jnp.arange(n_tiles, dtype=jnp.int32)
    tile_expert = jnp.minimum(jnp.sum(tile[:, None] >= tile_end[None, :], axis=1), N_EXPERTS - 1)
    pick = (tile_expert[:, None] == jnp.arange(N_EXPERTS, dtype=jnp.int32)[None, :]).astype(jnp.int32)
    per_tile = lambda tab: jnp.sum(pick[:, :, None] * tab.T[None, :, :], axis=1)
    first_tile = jnp.sum(pick * (tile_end - tiles_per_expert)[None, :], axis=1)
    vg = ((tile - first_tile)[:, None] * GRAN_PER_TILE
          + jnp.arange(GRAN_PER_TILE, dtype=jnp.int32)[None, :])[:, :, None]
    lo = per_tile(cum_excl)[:, None, :]
    hi = per_tile(cum_incl)[:, None, :]
    base = (jnp.arange(nb, dtype=jnp.int32)[None, :] * gran_per_block + per_tile(seg_start))[:, None, :]
    inside = jnp.logical_and(vg >= lo, vg < hi)
    phys = jnp.sum(jnp.where(inside, base + vg - lo, 0), axis=2)
    valid = jnp.logical_and(jnp.any(inside, axis=2), (tile < n_used[0])[:, None])
    gran = jnp.where(valid, phys, 0).astype(jnp.int32).reshape(-1)
    used = jnp.concatenate([n_used, jnp.sum(valid, axis=1)]).astype(jnp.int32)
    return tile_expert.astype(jnp.int32), used, gran


def _moe_ffn(x, w_router, e_gate, e_up, e_down, ln_g, ln_b):
    ri, rw, cnt, x_loc = _route_dispatch(x, w_router.T)
    tile_expert, n_used, gran = _granule_table(cnt[:, :, 0].astype(jnp.int32))
    y_loc = _experts(tile_expert, n_used, gran, x_loc,
                     e_gate.astype(BF16), e_up.astype(BF16), e_down.astype(BF16))
    return _combine_ln(y_loc, ri[2:2 + TOP_K].T, rw[:TOP_K].T, x,
                       ln_g.reshape(1, -1), ln_b.reshape(1, -1))


def kernel(x, w_in, gm_ws, gm_bs, gm_ln_g, gm_ln_b, gla_wa2, gla_ba, gla_norm_g, w_out, ln_mix_g, ln_mix_b, ffn_w_gate, ffn_w_up, ffn_w_down, router_w, exp_w_gate, exp_w_up, exp_w_down, ln_ffn_g, ln_ffn_b):
    batch, seq, d = x.shape
    h = x.reshape(batch * seq, d)
    for layer in range(DEPTH):
        h = _mixer(h, batch, seq, w_in[layer], gm_ws[layer], gm_bs[layer], gm_ln_g[layer],
                   gm_ln_b[layer], gla_wa2[layer], gla_ba[layer], gla_norm_g[layer], w_out[layer],
                   ln_mix_g[layer], ln_mix_b[layer])
        i = layer // 2
        if layer % 2 == 0:
            h = _swiglu_ln(h, ffn_w_gate[i].astype(BF16), ffn_w_up[i].astype(BF16),
                           ffn_w_down[i].astype(BF16), ln_ffn_g[layer].reshape(1, -1),
                           ln_ffn_b[layer].reshape(1, -1))
        else:
            h = _moe_ffn(h, router_w[i], exp_w_gate[i], exp_w_up[i], exp_w_down[i],
                         ln_ffn_g[layer], ln_ffn_b[layer])
    return h.reshape(batch, seq, d)
```

```python
import functools

import jax
import jax.numpy as jnp
from jax import lax
from jax.experimental import pallas as pl
from jax.experimental.pallas import tpu as pltpu

F32 = jnp.float32
BF16 = jnp.bfloat16

D_MODEL = 1024
DEPTH = 2
GM_HEADS = 4
GM_WIDTH = 512
GM_HEAD_DIM = 128
GM_CHUNK = 128
GLA_HEADS = 4
GLA_WIDTH = 512
GLA_DV = 128
GLA_KEY_WIDTH = 256
GLA_DK = 64
GLA_GATE_RANK = 16
GLA_TAU = 16.0
GLA_CHUNK = 64
PROJ_MAIN = 2 * GM_WIDTH + 2 * GLA_KEY_WIDTH + 2 * GLA_WIDTH
N_EXPERTS = 8
TOP_K = 2
LN_EPS = 1e-5
RMS_EPS = 1e-6
DEEPNORM_ALPHA = (2 * DEPTH) ** 0.25

V7X_VMEM_LIMIT_BYTES = 56 * 1024 * 1024
LANE = 128
SUBLANE = 8
MXU_N = 256

ROW_TILE = 512
FFN_ROW_TILE = 1024
DENSE_FF_TILE = 256
MOE_FF_TILE = 512
MOE_BLOCK = 512
GRANULE = 16
LOC_ROWS = -(-(TOP_K * MOE_BLOCK + N_EXPERTS * (GRANULE - 1)) // LANE) * LANE
COMBINE_SUB = 256
MIXER_HEAD_FILLERS = 6
MIXER_TAIL_FILLERS = 3
GRAN_UNROLL = 4
GRAN_PER_TILE = FFN_ROW_TILE // GRANULE


def _params(*sem, **kw):
    return pltpu.CompilerParams(dimension_semantics=sem, vmem_limit_bytes=V7X_VMEM_LIMIT_BYTES, **kw)


def _gelu(x):
    return 0.5 * x * (1.0 + lax.erf(x * (2.0 ** -0.5)))


def _layer_norm(y, g, b):
    mu = jnp.mean(y, axis=-1, keepdims=True)
    d = y - mu
    var = jnp.mean(d * d, axis=-1, keepdims=True)
    return d * lax.rsqrt(var + LN_EPS) * g + b


def _fold_kernel(w1_ref, w2_ref, o_ref):
    acc = jnp.zeros(o_ref.shape, F32)
    for r in range(GLA_GATE_RANK):
        acc = acc + w1_ref[:, r:r + 1] * w2_ref[r:r + 1, :]
    o_ref[...] = acc


def _fold_gate(w1, w2):
    return pl.pallas_call(
        _fold_kernel,
        out_shape=jax.ShapeDtypeStruct((w1.shape[0], w2.shape[1]), F32),
        name="fold_gate",
    )(w1, w2)


def _inproj_items(x_ref, w_ref, p_ref, z_ref):
    cache = {}

    def xb():
        if "xb" not in cache:
            cache["xb"] = x_ref[...].astype(BF16)
        return cache["xb"]

    def block(n):
        cols = slice(n * MXU_N, (n + 1) * MXU_N)

        def run():
            p_ref[:, cols] = jnp.dot(xb(), w_ref[:, cols], preferred_element_type=F32).astype(BF16)
        return run

    def gate_logits():
        z_ref[...] = jnp.dot(xb(), w_ref[:, PROJ_MAIN:], preferred_element_type=F32)

    return [block(n) for n in range(PROJ_MAIN // MXU_N)] + [gate_logits]


def _gmlp_items(uv_ref, w_ref, bcol_ref, g_ref, b_ref, o_ref):
    rows = uv_ref.shape[0]

    def head(h):
        cols = slice(h * GM_HEAD_DIM, (h + 1) * GM_HEAD_DIM)
        vcols = slice(GM_WIDTH + h * GM_HEAD_DIM, GM_WIDTH + (h + 1) * GM_HEAD_DIM)

        def run():
            r = lax.broadcasted_iota(jnp.int32, (GM_CHUNK, GM_CHUNK), 0)
            c = lax.broadcasted_iota(jnp.int32, (GM_CHUNK, GM_CHUNK), 1)
            wm = jnp.where(r >= c, w_ref[h], 0.0).astype(BF16)
            v = _gelu(uv_ref[:, vcols].astype(F32))
            vn = _layer_norm(v, g_ref[:, cols], b_ref[:, cols]).astype(BF16)
            u = _gelu(uv_ref[:, cols].astype(F32))
            for ci in range(rows // GM_CHUNK):
                rs = slice(ci * GM_CHUNK, (ci + 1) * GM_CHUNK)
                s = jnp.dot(wm, vn[rs], preferred_element_type=F32) + bcol_ref[h]
                o_ref[rs, cols] = (u[rs] * s).astype(BF16)
        return run

    return [head(h) for h in range(GM_HEADS)]


def _gla_items(q_ref, k_ref, v_ref, g_ref, z_ref, ba_ref, ng_ref, o_ref,
               st_ref, qd_ref, ki_ref, kd_ref, dec_ref, raw_ref):
    rows = q_ref.shape[0]
    n_chunks = rows // GLA_CHUNK
    shift = GLA_CHUNK.bit_length() - 1
    nt = (((1,), (1,)), ((), ()))
    tn = (((0,), (0,)), ((), ()))

    def prep():
        r = lax.broadcasted_iota(jnp.int32, (rows, rows), 0)
        c = lax.broadcasted_iota(jnp.int32, (rows, rows), 1)
        ones_tril = jnp.where(jnp.logical_and((r >> shift) == (c >> shift), r >= c),
                              1.0, 0.0).astype(BF16)
        z = z_ref[...] + ba_ref[...]
        la = (jnp.minimum(z, 0.0) - jnp.log1p(jnp.exp(-jnp.abs(z)))) * (1.0 / GLA_TAU)
        la_hi = la.astype(BF16)
        la_lo = (la - la_hi.astype(F32)).astype(BF16)
        b = (jnp.dot(ones_tril, la_hi, preferred_element_type=F32)
             + jnp.dot(ones_tril, la_lo, preferred_element_type=F32))
        b3 = b.reshape(n_chunks, GLA_CHUNK, GLA_KEY_WIDTH)
        b_last = b3[:, GLA_CHUNK - 1:GLA_CHUNK, :]
        k = k_ref[...].astype(F32)
        qd_ref[...] = (q_ref[...].astype(F32) * (GLA_DK ** -0.5) * jnp.exp(b)).astype(BF16)
        ki_ref[...] = (k * jnp.exp(-b)).astype(BF16)
        kd_ref[...] = (k * jnp.exp(b_last - b3).reshape(rows, GLA_KEY_WIDTH)).astype(BF16)
        dec_ref[...] = jnp.exp(b_last).reshape(n_chunks, GLA_KEY_WIDTH)

    def chunk(ci):
        rs = slice(ci * GLA_CHUNK, (ci + 1) * GLA_CHUNK)

        def run(between=None):
            rc = lax.broadcasted_iota(jnp.int32, (GLA_CHUNK, GLA_CHUNK), 0)
            cc = lax.broadcasted_iota(jnp.int32, (GLA_CHUNK, GLA_CHUNK), 1)
            causal = rc >= cc
            q_dec = qd_ref[rs, :]
            k_inv = ki_ref[rs, :]
            k_dec = kd_ref[rs, :]
            decay = dec_ref[ci:ci + 1, :]
            head_cols = [slice(h * GLA_DK, (h + 1) * GLA_DK) for h in range(GLA_HEADS)]
            all_scores = [lax.dot_general(q_dec[:, ks], k_inv[:, ks], nt, preferred_element_type=F32)
                          for ks in head_cols]
            if between is not None:
                between()
            for h in range(GLA_HEADS):
                ks = head_cols[h]
                vs = slice(h * GLA_DV, (h + 1) * GLA_DV)
                scores = jnp.where(causal, all_scores[h], 0.0).astype(BF16)
                vh = v_ref[rs, vs]
                state_t = st_ref[h]
                o = (jnp.dot(scores, vh, preferred_element_type=F32)
                     + lax.dot_general(q_dec[:, ks], state_t.astype(BF16), nt,
                                       preferred_element_type=F32))
                st_ref[h] = (state_t * decay[:, ks]
                             + lax.dot_general(vh, k_dec[:, ks], tn, preferred_element_type=F32))
                raw_ref[rs, vs] = o
        return run

    def finish():
        for h in range(GLA_HEADS):
            vs = slice(h * GLA_DV, (h + 1) * GLA_DV)
            o = raw_ref[:, vs]
            o = o * lax.rsqrt(jnp.mean(o * o, axis=-1, keepdims=True) + RMS_EPS) * ng_ref[:, vs]
            gate = g_ref[:, vs].astype(F32)
            o_ref[:, vs] = (o * (gate * jax.nn.sigmoid(gate))).astype(BF16)

    return prep, [chunk(ci) for ci in range(n_chunks)], finish


def _mixer_kernel(xa_ref, xc_ref, win_ref, wout_ref, lng_ref, lnb_ref,
                  gw_ref, gbcol_ref, gg_ref, gb_ref, ba_ref, ng_ref, o_ref,
                  proj0, proj1, z0, z1, ab0, ab1, st_ref, *gla_scratch, tiles_per_seq):
    t = pl.program_id(0)
    proj_scr, z_scr, ab_scr = (proj0, proj1), (z0, z1), (ab0, ab1)

    @pl.when(t == 0)
    def _():
        for ref in (proj0, proj1, z0, z1, ab0, ab1, st_ref):
            ref[...] = jnp.zeros(ref.shape, ref.dtype)

    @pl.when(lax.rem(t + tiles_per_seq - 1, tiles_per_seq) == 0)
    def _():
        st_ref[...] = jnp.zeros(st_ref.shape, F32)

    q0 = 2 * GM_WIDTH
    k0 = q0 + GLA_KEY_WIDTH
    v0 = k0 + GLA_KEY_WIDTH
    g0 = v0 + GLA_WIDTH

    def stages(proj_a, z_a, proj_b, z_b, ab_b, ab_c):
        inproj = _inproj_items(xa_ref, win_ref, proj_a, z_a)
        gla_prep, gla_chunks, gla_finish = _gla_items(
            proj_b.at[:, q0:k0], proj_b.at[:, k0:v0], proj_b.at[:, v0:g0],
            proj_b.at[:, g0:PROJ_MAIN], z_b, ba_ref, ng_ref, ab_b.at[:, GM_WIDTH:],
            st_ref, *gla_scratch)
        gmlp = _gmlp_items(proj_b.at[:, :q0], gw_ref, gbcol_ref, gg_ref, gb_ref,
                           ab_b.at[:, :GM_WIDTH])
        outproj, outproj_finish = _outproj_items(ab_c.at[:, :GM_WIDTH], ab_c.at[:, GM_WIDTH:],
                                                 wout_ref, xc_ref, lng_ref, lnb_ref, o_ref)
        fillers = inproj + outproj
        head, tail = MIXER_HEAD_FILLERS, len(fillers) - MIXER_TAIL_FILLERS
        for run in fillers[:head // 2]:
            run()
        gla_prep()
        for run in fillers[head // 2:head]:
            run()
        done = head
        for i, run_chunk in enumerate(gla_chunks):
            target = head + (i + 1) * (tail - head) // len(gla_chunks)
            mid, done = fillers[done:target], target
            run_chunk(between=lambda mid=mid: [run() for run in mid])
        late = fillers[tail:] + [gla_finish]
        for i, run_head in enumerate(gmlp):
            for run in late[i * len(late) // len(gmlp):(i + 1) * len(late) // len(gmlp)]:
                run()
            run_head()
        outproj_finish()

    for parity in range(2):
        @pl.when(t % 2 == parity)
        def _(a=parity, b=1 - parity):
            stages(proj_scr[a], z_scr[a], proj_scr[b], z_scr[b], ab_scr[b], ab_scr[a])


def _mixer_call(x, w_ext, w_out, ln_g, ln_b, gm_ws, gm_bcol, gm_ln_g, gm_ln_b, ba, norm_g, seq):
    t, d = x.shape
    n_tiles = t // ROW_TILE
    kw, vw = GLA_KEY_WIDTH, GLA_WIDTH
    const2 = lambda i: (0, 0)
    const3 = lambda i: (0, 0, 0)
    resident = pl.Buffered(1)
    tile_c = lambda i: (jnp.maximum(i - 2, 0), 0)
    return pl.pallas_call(
        functools.partial(_mixer_kernel, tiles_per_seq=seq // ROW_TILE),
        grid=(n_tiles + 2,),
        in_specs=[pl.BlockSpec((ROW_TILE, d), lambda i: (jnp.minimum(i, n_tiles - 1), 0)),
                  pl.BlockSpec((ROW_TILE, d), tile_c),
                  pl.BlockSpec(w_ext.shape, const2, pipeline_mode=resident),
                  pl.BlockSpec(w_out.shape, const2, pipeline_mode=resident),
                  pl.BlockSpec((1, d), const2),
                  pl.BlockSpec((1, d), const2),
                  pl.BlockSpec((GM_HEADS, GM_CHUNK, GM_CHUNK), const3),
                  pl.BlockSpec((GM_HEADS, GM_CHUNK, 1), const3),
                  pl.BlockSpec((1, GM_WIDTH), const2),
                  pl.BlockSpec((1, GM_WIDTH), const2),
                  pl.BlockSpec((1, kw), const2),
                  pl.BlockSpec((1, vw), const2)],
        out_specs=pl.BlockSpec((ROW_TILE, d), tile_c),
        out_shape=jax.ShapeDtypeStruct((t, d), F32),
        scratch_shapes=[pltpu.VMEM((ROW_TILE, PROJ_MAIN), BF16),
                        pltpu.VMEM((ROW_TILE, PROJ_MAIN), BF16),
                        pltpu.VMEM((ROW_TILE, kw), F32),
                        pltpu.VMEM((ROW_TILE, kw), F32),
                        pltpu.VMEM((ROW_TILE, GM_WIDTH + vw), BF16),
                        pltpu.VMEM((ROW_TILE, GM_WIDTH + vw), BF16),
                        pltpu.VMEM((GLA_HEADS, GLA_DV, GLA_DK), F32),
                        pltpu.VMEM((ROW_TILE, kw), BF16),
                        pltpu.VMEM((ROW_TILE, kw), BF16),
                        pltpu.VMEM((ROW_TILE, kw), BF16),
                        pltpu.VMEM((ROW_TILE // GLA_CHUNK, kw), F32),
                        pltpu.VMEM((ROW_TILE, vw), F32)],
        compiler_params=_params("arbitrary"),
        name="mixer",
    )(x, x, w_ext, w_out, ln_g, ln_b, gm_ws, gm_bcol, gm_ln_g, gm_ln_b, ba, norm_g)


def _outproj_items(a_ref, b_ref, w_ref, x_ref, g_ref, be_ref, o_ref):
    def block(n):
        cols = slice(n * MXU_N, (n + 1) * MXU_N)

        def run():
            h = (jnp.dot(a_ref[...], w_ref[:GM_WIDTH, cols], preferred_element_type=F32)
                 + jnp.dot(b_ref[...], w_ref[GM_WIDTH:, cols], preferred_element_type=F32))
            o_ref[:, cols] = DEEPNORM_ALPHA * x_ref[:, cols] + h
        return run

    def finish():
        o_ref[...] = _layer_norm(o_ref[...], g_ref[...], be_ref[...])

    return [block(n) for n in range(o_ref.shape[1] // MXU_N)], finish


def _swiglu_ln_kernel(x_ref, wg_ref, wu_ref, wd_ref, g_ref, b_ref, o_ref, hid_ref):
    xb = x_ref[...].astype(BF16)
    for n in range(wg_ref.shape[1] // DENSE_FF_TILE):
        cols = slice(n * DENSE_FF_TILE, (n + 1) * DENSE_FF_TILE)
        gate = jnp.dot(xb, wg_ref[:, cols], preferred_element_type=F32)
        up = jnp.dot(xb, wu_ref[:, cols], preferred_element_type=F32)
        hid_ref[:, cols] = (gate * jax.nn.sigmoid(gate) * up).astype(BF16)
    f = jnp.dot(hid_ref[...], wd_ref[...], preferred_element_type=F32)
    o_ref[...] = _layer_norm(DEEPNORM_ALPHA * x_ref[...] + f, g_ref[...], b_ref[...])


def _swiglu_ln(x, wg, wu, wd, g, b):
    rows, d = x.shape
    ff = wg.shape[1]
    resident = pl.Buffered(1)
    return pl.pallas_call(
        _swiglu_ln_kernel,
        grid=(rows // FFN_ROW_TILE,),
        in_specs=[pl.BlockSpec((FFN_ROW_TILE, d), lambda i: (i, 0)),
                  pl.BlockSpec((d, ff), lambda i: (0, 0), pipeline_mode=resident),
                  pl.BlockSpec((d, ff), lambda i: (0, 0), pipeline_mode=resident),
                  pl.BlockSpec((ff, d), lambda i: (0, 0), pipeline_mode=resident),
                  pl.BlockSpec((1, d), lambda i: (0, 0)),
                  pl.BlockSpec((1, d), lambda i: (0, 0))],
        out_specs=pl.BlockSpec((FFN_ROW_TILE, d), lambda i: (i, 0)),
        out_shape=jax.ShapeDtypeStruct((rows, d), F32),
        scratch_shapes=[pltpu.VMEM((FFN_ROW_TILE, ff), BF16)],
        compiler_params=_params("parallel"),
        name="swiglu_ln",
    )(x, wg, wu, wd, g, b)


def _split_bf16(v):
    hi = v.astype(BF16)
    return hi, (v - hi.astype(F32)).astype(BF16)


def _router_kernel(x_ref, wr_ref, ri_ref, rw_ref, cnt_ref):
    tb = x_ref.shape[0]
    x_hi, x_lo = _split_bf16(x_ref[...])
    w_hi, w_lo = _split_bf16(wr_ref[...])
    nt = (((1,), (1,)), ((), ()))
    logits = (lax.dot_general(w_hi, x_hi, nt, preferred_element_type=F32)
              + lax.dot_general(w_hi, x_lo, nt, preferred_element_type=F32)
              + lax.dot_general(w_lo, x_hi, nt, preferred_element_type=F32))
    e_iota = lax.broadcasted_iota(jnp.int32, (N_EXPERTS, tb), 0)
    m0 = jnp.max(logits, axis=0, keepdims=True)
    i0 = jnp.min(jnp.where(logits == m0, e_iota, N_EXPERTS), axis=0, keepdims=True)
    rest = jnp.where(e_iota == i0, -jnp.inf, logits)
    m1 = jnp.max(rest, axis=0, keepdims=True)
    i1 = jnp.min(jnp.where(rest == m1, e_iota, N_EXPERTS), axis=0, keepdims=True)
    ex = jnp.exp(m1 - m0)
    w0 = 1.0 / (1.0 + ex)
    w1 = ex / (1.0 + ex)
    oh0 = e_iota == i0
    oh1 = e_iota == i1
    onehot = jnp.where(jnp.logical_or(oh0, oh1), 1.0, 0.0)
    tr = lax.broadcasted_iota(jnp.int32, (tb, tb), 0)
    tc = lax.broadcasted_iota(jnp.int32, (tb, tb), 1)
    before = jnp.where(tr < tc, 1.0, 0.0).astype(BF16)
    rank = jnp.dot(onehot.astype(BF16), before, preferred_element_type=F32)
    count = jnp.broadcast_to(jnp.sum(onehot, axis=1, keepdims=True), (N_EXPERTS, LANE))
    cap = jnp.ceil(count * (1.0 / GRANULE)) * GRANULE
    e_sub = lax.broadcasted_iota(jnp.int32, (N_EXPERTS, LANE), 0)
    seg = jnp.zeros((N_EXPERTS, LANE), F32)
    for e in range(N_EXPERTS - 1):
        seg = seg + jnp.where(e_sub > e, cap[e:e + 1, :], 0.0)
    row = rank + seg[:, 0:1]
    row0 = jnp.sum(jnp.where(oh0, row, 0.0), axis=0, keepdims=True)
    row1 = jnp.sum(jnp.where(oh1, row, 0.0), axis=0, keepdims=True)
    cnt_ref[...] = count
    ri_ref[...] = jnp.zeros(ri_ref.shape, jnp.int32)
    ri_ref[0:1, :] = i0
    ri_ref[1:2, :] = i1
    ri_ref[2:3, :] = row0.astype(jnp.int32)
    ri_ref[3:4, :] = row1.astype(jnp.int32)
    rw_ref[...] = jnp.zeros(rw_ref.shape, F32)
    rw_ref[0:1, :] = w0
    rw_ref[1:2, :] = w1
    return x_hi


def _route_dispatch_kernel(x_ref, wr_ref, ri_ref, rw_ref, cnt_ref, o_ref):
    xb = _router_kernel(x_ref, wr_ref, ri_ref, rw_ref, cnt_ref)
    rows, tb = o_ref.shape[0], x_ref.shape[0]
    r = lax.broadcasted_iota(jnp.int32, (rows, tb), 0)
    sel = jnp.logical_or(r == ri_ref[2:3, :], r == ri_ref[3:4, :])
    perm = jnp.where(sel, 1.0, 0.0).astype(BF16)
    o_ref[...] = jnp.dot(perm, xb, preferred_element_type=F32).astype(BF16)


def _route_dispatch(x, wr_t):
    t, d = x.shape
    nb = t // MOE_BLOCK
    return pl.pallas_call(
        _route_dispatch_kernel,
        grid=(nb,),
        in_specs=[pl.BlockSpec((MOE_BLOCK, d), lambda i: (i, 0)),
                  pl.BlockSpec((N_EXPERTS, d), lambda i: (0, 0))],
        out_specs=[pl.BlockSpec((SUBLANE, MOE_BLOCK), lambda i: (0, i)),
                   pl.BlockSpec((SUBLANE, MOE_BLOCK), lambda i: (0, i)),
                   pl.BlockSpec((None, N_EXPERTS, LANE), lambda i: (i, 0, 0)),
                   pl.BlockSpec((LOC_ROWS, d), lambda i: (i, 0))],
        out_shape=[jax.ShapeDtypeStruct((SUBLANE, t), jnp.int32),
                   jax.ShapeDtypeStruct((SUBLANE, t), F32),
                   jax.ShapeDtypeStruct((nb, N_EXPERTS, LANE), F32),
                   jax.ShapeDtypeStruct((nb * LOC_ROWS, d), BF16)],
        compiler_params=_params("parallel"),
        name="route_dispatch",
    )(x, wr_t)


def _experts_kernel(te_ref, nu_ref, gran_ref, x_hbm, wg_hbm, wu_hbm, wd_hbm, y_hbm,
                    xbuf, ybuf, hid_ref, wg_buf, wu_buf, wd_buf, sem_in, sem_out, sem_w):
    i = pl.program_id(0)
    n_tiles = pl.num_programs(0)
    n_used = nu_ref[0]
    n_chunks = wg_buf.shape[0]
    expert = te_ref[i]
    prev_expert = te_ref[jnp.maximum(i - 1, 0)]
    next_expert = te_ref[jnp.minimum(i + 1, n_tiles - 1)]
    first_of_expert = jnp.logical_or(i == 0, prev_expert != expert)
    refill = jnp.logical_and(i + 1 < n_used, next_expert != expert)

    def up_copies(e, n):
        cols = pl.ds(n * MOE_FF_TILE, MOE_FF_TILE)
        return (pltpu.make_async_copy(wg_hbm.at[e, :, cols], wg_buf.at[n], sem_w.at[n]),
                pltpu.make_async_copy(wu_hbm.at[e, :, cols], wu_buf.at[n], sem_w.at[n]))

    def down_copy(e):
        return pltpu.make_async_copy(wd_hbm.at[e], wd_buf, sem_w.at[n_chunks])

    def for_granules(tile, fn):
        def one(s):
            g = gran_ref[tile * GRAN_PER_TILE + s]
            fn(pl.ds(pl.multiple_of(g * GRANULE, GRANULE), GRANULE),
               pl.ds(pl.multiple_of(s * GRANULE, GRANULE), GRANULE))

        def group(p, carry):
            for u in range(GRAN_UNROLL):
                one(p * GRAN_UNROLL + u)
            return carry

        def single(s, carry):
            one(s)
            return carry

        filled = nu_ref[1 + tile]
        n_groups = filled // GRAN_UNROLL
        lax.fori_loop(0, n_groups, group, 0)
        lax.fori_loop(n_groups * GRAN_UNROLL, filled, single, 0)

    def gather(slot):
        return lambda hbm_rows, tile_rows: pltpu.make_async_copy(
            x_hbm.at[hbm_rows], xbuf.at[slot, tile_rows], sem_in.at[slot])

    def scatter(hbm_rows, tile_rows):
        return pltpu.make_async_copy(ybuf.at[tile_rows], y_hbm.at[hbm_rows], sem_out.at[0])

    @pl.when(i < n_used)
    def _():
        slot = i % 2

        @pl.when(i == 0)
        def _():
            for n in range(n_chunks):
                for cp in up_copies(expert, n):
                    cp.start()
            down_copy(expert).start()
            xbuf[...] = jnp.zeros(xbuf.shape, BF16)
            for_granules(0, lambda h, r: gather(0)(h, r).start())

        for_granules(i, lambda h, r: gather(slot)(h, r).wait())

        @pl.when(i + 1 < n_used)
        def _():
            for_granules(i + 1, lambda h, r: gather(1 - slot)(h, r).start())

        def hidden(sync_weights):
            xb = xbuf[slot]
            for n in range(n_chunks):
                if sync_weights:
                    @pl.when(first_of_expert)
                    def _():
                        for cp in up_copies(expert, n):
                            cp.wait()

                gate = jnp.dot(xb, wg_buf[n], preferred_element_type=F32)
                up = jnp.dot(xb, wu_buf[n], preferred_element_type=F32)
                hid_ref[:, n * MOE_FF_TILE:(n + 1) * MOE_FF_TILE] = (
                    gate * jax.nn.sigmoid(gate) * up).astype(BF16)

                if sync_weights:
                    @pl.when(refill)
                    def _():
                        for cp in up_copies(next_expert, n):
                            cp.start()

        edge_tile = jnp.logical_or(first_of_expert, refill)
        pl.when(edge_tile)(functools.partial(hidden, True))
        pl.when(jnp.logical_not(edge_tile))(functools.partial(hidden, False))

        @pl.when(first_of_expert)
        def _():
            down_copy(expert).wait()

        @pl.when(i > 0)
        def _():
            for_granules(i - 1, lambda h, r: scatter(h, r).wait())

        ybuf[...] = jnp.dot(hid_ref[...], wd_buf[...], preferred_element_type=F32).astype(BF16)

        @pl.when(refill)
        def _():
            down_copy(next_expert).start()

        for_granules(i, lambda h, r: scatter(h, r).start())

        @pl.when(i == n_used - 1)
        def _():
            for_granules(i, lambda h, r: scatter(h, r).wait())


def _experts(tile_expert, used, gran, x_loc, wg, wu, wd):
    rows, d = x_loc.shape
    ff = wg.shape[2]
    n_tiles = tile_expert.shape[0]
    nj = ff // MOE_FF_TILE
    hbm = pl.BlockSpec(memory_space=pl.ANY)
    return pl.pallas_call(
        _experts_kernel,
        grid_spec=pltpu.PrefetchScalarGridSpec(
            num_scalar_prefetch=3,
            grid=(n_tiles,),
            in_specs=[hbm, hbm, hbm, hbm],
            out_specs=hbm,
            scratch_shapes=[pltpu.VMEM((2, FFN_ROW_TILE, d), BF16),
                            pltpu.VMEM((FFN_ROW_TILE, d), BF16),
                            pltpu.VMEM((FFN_ROW_TILE, ff), BF16),
                            pltpu.VMEM((nj, d, MOE_FF_TILE), BF16),
                            pltpu.VMEM((nj, d, MOE_FF_TILE), BF16),
                            pltpu.VMEM((ff, d), BF16),
                            pltpu.SemaphoreType.DMA((2,)),
                            pltpu.SemaphoreType.DMA((1,)),
                            pltpu.SemaphoreType.DMA((nj + 1,))]),
        out_shape=jax.ShapeDtypeStruct((rows, d), BF16),
        input_output_aliases={3: 0},
        compiler_params=_params("arbitrary"),
        name="swiglu_experts",
    )(tile_expert, used, gran, x_loc, wg, wu, wd)


def _combine_ln_kernel(y_ref, row_ref, w_ref, x_ref, g_ref, b_ref, o_ref):
    tb, rows = x_ref.shape[0], y_ref.shape[0]
    for k in range(tb // COMBINE_SUB):
        ts = slice(k * COMBINE_SUB, (k + 1) * COMBINE_SUB)
        c = lax.broadcasted_iota(jnp.int32, (COMBINE_SUB, rows), 1)
        q = (jnp.where(c == row_ref[ts, 0:1], w_ref[ts, 0:1], 0.0)
             + jnp.where(c == row_ref[ts, 1:2], w_ref[ts, 1:2], 0.0))
        f = jnp.dot(q.astype(BF16), y_ref[...], preferred_element_type=F32)
        o_ref[ts, :] = _layer_norm(DEEPNORM_ALPHA * x_ref[ts, :] + f, g_ref[...], b_ref[...])


def _combine_ln(y_loc, row_col, w_col, x, g, b):
    t, d = x.shape
    return pl.pallas_call(
        _combine_ln_kernel,
        grid=(t // MOE_BLOCK,),
        in_specs=[pl.BlockSpec((LOC_ROWS, d), lambda i: (i, 0)),
                  pl.BlockSpec((MOE_BLOCK, TOP_K), lambda i: (i, 0)),
                  pl.BlockSpec((MOE_BLOCK, TOP_K), lambda i: (i, 0)),
                  pl.BlockSpec((MOE_BLOCK, d), lambda i: (i, 0)),
                  pl.BlockSpec((1, d), lambda i: (0, 0)),
                  pl.BlockSpec((1, d), lambda i: (0, 0))],
        out_specs=pl.BlockSpec((MOE_BLOCK, d), lambda i: (i, 0)),
        out_shape=jax.ShapeDtypeStruct((t, d), F32),
        compiler_params=_params("parallel"),
        name="combine_ln",
    )(y_loc, row_col, w_col, x, g, b)


def _mixer(x, batch, seq, w_in, gm_ws, gm_bs, gm_ln_g, gm_ln_b, gla_wa2, gla_ba, gla_norm_g,
           w_out, ln_g, ln_b):
    w_gate = _fold_gate(w_in[:, PROJ_MAIN:], gla_wa2)
    w_ext = jnp.concatenate([w_in[:, :PROJ_MAIN], w_gate], axis=1).astype(BF16)
    del batch
    return _mixer_call(x, w_ext, w_out.astype(BF16), ln_g.reshape(1, -1), ln_b.reshape(1, -1),
                       gm_ws, gm_bs.reshape(GM_HEADS, GM_CHUNK, 1),
                       gm_ln_g.reshape(1, -1), gm_ln_b.reshape(1, -1),
                       gla_ba.reshape(1, -1), gla_norm_g.reshape(1, -1), seq)


def _granule_table(counts):
    nb = counts.shape[0]
    n_tiles = (nb * (TOP_K * MOE_BLOCK + N_EXPERTS * (GRANULE - 1))) // FFN_ROW_TILE + N_EXPERTS
    gran_per_block = LOC_ROWS // GRANULE
    g = (counts + GRANULE - 1) // GRANULE
    seg_start = jnp.cumsum(g, axis=1) - g
    cum_incl = jnp.cumsum(g, axis=0)
    cum_excl = cum_incl - g
    total = cum_incl[-1]
    tiles_per_expert = (total + GRAN_PER_TILE - 1) // GRAN_PER_TILE
    tile_end = jnp.cumsum(tiles_per_expert)
    n_used = tile_end[-1:]
    tile = jnp.arange(n_tiles, dtype=jnp.int32)
    tile_expert = jnp.minimum(jnp.sum(tile[:, None] >= tile_end[None, :], axis=1), N_EXPERTS - 1)
    pick = (tile_expert[:, None] == jnp.arange(N_EXPERTS, dtype=jnp.int32)[None, :]).astype(jnp.int32)
    per_tile = lambda tab: jnp.sum(pick[:, :, None] * tab.T[None, :, :], axis=1)
    first_tile = jnp.sum(pick * (tile_end - tiles_per_expert)[None, :], axis=1)
    vg = ((tile - first_tile)[:, None] * GRAN_PER_TILE
          + jnp.arange(GRAN_PER_TILE, dtype=jnp.int32)[None, :])[:, :, None]
    lo = per_tile(cum_excl)[:, None, :]
    hi = per_tile(cum_incl)[:, None, :]
    base = (jnp.arange(nb, dtype=jnp.int32)[None, :] * gran_per_block + per_tile(seg_start))[:, None, :]
    inside = jnp.logical_and(vg >= lo, vg < hi)
    phys = jnp.sum(jnp.where(inside, base + vg - lo, 0), axis=2)
    valid = jnp.logical_and(jnp.any(inside, axis=2), (tile < n_used[0])[:, None])
    gran = jnp.where(valid, phys, 0).astype(jnp.int32).reshape(-1)
    used = jnp.concatenate([n_used, jnp.sum(valid, axis=1)]).astype(jnp.int32)
    return tile_expert.astype(jnp.int32), used, gran


def _moe_ffn(x, w_router, e_gate, e_up, e_down, ln_g, ln_b):
    ri, rw, cnt, x_loc = _route_dispatch(x, w_router.T)
    tile_expert, n_used, gran = _granule_table(cnt[:, :, 0].astype(jnp.int32))
    y_loc = _experts(tile_expert, n_used, gran, x_loc,
                     e_gate.astype(BF16), e_up.astype(BF16), e_down.astype(BF16))
    return _combine_ln(y_loc, ri[2:2 + TOP_K].T, rw[:TOP_K].T, x,
                       ln_g.reshape(1, -1), ln_b.reshape(1, -1))


def kernel(x, w_in, gm_ws, gm_bs, gm_ln_g, gm_ln_b, gla_wa2, gla_ba, gla_norm_g, w_out, ln_mix_g, ln_mix_b, ffn_w_gate, ffn_w_up, ffn_w_down, router_w, exp_w_gate, exp_w_up, exp_w_down, ln_ffn_g, ln_ffn_b):
    batch, seq, d = x.shape
    h = x.reshape(batch * seq, d)
    for layer in range(DEPTH):
        h = _mixer(h, batch, seq, w_in[layer], gm_ws[layer], gm_bs[layer], gm_ln_g[layer],
                   gm_ln_b[layer], gla_wa2[layer], gla_ba[layer], gla_norm_g[layer], w_out[layer],
                   ln_mix_g[layer], ln_mix_b[layer])
        i = layer // 2
        if layer % 2 == 0:
            h = _swiglu_ln(h, ffn_w_gate[i].astype(BF16), ffn_w_up[i].astype(BF16),
                           ffn_w_down[i].astype(BF16), ln_ffn_g[layer].reshape(1, -1),
                           ln_ffn_b[layer].reshape(1, -1))
        else:
            h = _moe_ffn(h, router_w[i], exp_w_gate[i], exp_w_up[i], exp_w_down[i],
                         ln_ffn_g[layer], ln_ffn_b[layer])
    return h.reshape(batch, seq, d)
```

```python
import functools

import jax
import jax.numpy as jnp
from jax import lax
from jax.experimental import pallas as pl
from jax.experimental.pallas import tpu as pltpu

F32 = jnp.float32
BF16 = jnp.bfloat16

D_MODEL = 1024
DEPTH = 2
GM_HEADS = 4
GM_WIDTH = 512
GM_HEAD_DIM = 128
GM_CHUNK = 128
GLA_HEADS = 4
GLA_WIDTH = 512
GLA_DV = 128
GLA_KEY_WIDTH = 256
GLA_DK = 64
GLA_GATE_RANK = 16
GLA_TAU = 16.0
GLA_CHUNK = 64
PROJ_MAIN = 2 * GM_WIDTH + 2 * GLA_KEY_WIDTH + 2 * GLA_WIDTH
N_EXPERTS = 8
TOP_K = 2
LN_EPS = 1e-5
RMS_EPS = 1e-6
DEEPNORM_ALPHA = (2 * DEPTH) ** 0.25

V7X_VMEM_LIMIT_BYTES = 56 * 1024 * 1024
LANE = 128
SUBLANE = 8
MXU_N = 256

ROW_TILE = 512
FFN_ROW_TILE = 1024
DENSE_FF_TILE = 256
MOE_FF_TILE = 512
MOE_BLOCK = 512
GRANULE = 16
LOC_ROWS = -(-(TOP_K * MOE_BLOCK + N_EXPERTS * (GRANULE - 1)) // LANE) * LANE
COMBINE_SUB = 256
MIXER_HEAD_FILLERS = 6
MIXER_TAIL_FILLERS = 3
GRAN_UNROLL = 4
GRAN_PER_TILE = FFN_ROW_TILE // GRANULE


def _params(*sem, **kw):
    return pltpu.CompilerParams(dimension_semantics=sem, vmem_limit_bytes=V7X_VMEM_LIMIT_BYTES, **kw)


def _gelu(x):
    return 0.5 * x * (1.0 + lax.erf(x * (2.0 ** -0.5)))


def _layer_norm(y, g, b):
    mu = jnp.mean(y, axis=-1, keepdims=True)
    d = y - mu
    var = jnp.mean(d * d, axis=-1, keepdims=True)
    return d * lax.rsqrt(var + LN_EPS) * g + b


def _fold_kernel(w1_ref, w2_ref, o_ref):
    acc = jnp.zeros(o_ref.shape, F32)
    for r in range(GLA_GATE_RANK):
        acc = acc + w1_ref[:, r:r + 1] * w2_ref[r:r + 1, :]
    o_ref[...] = acc


def _fold_gate(w1, w2):
    return pl.pallas_call(
        _fold_kernel,
        out_shape=jax.ShapeDtypeStruct((w1.shape[0], w2.shape[1]), F32),
        name="fold_gate",
    )(w1, w2)


def _inproj_items(x_ref, w_ref, p_ref, z_ref):
    cache = {}

    def xb():
        if "xb" not in cache:
            cache["xb"] = x_ref[...].astype(BF16)
        return cache["xb"]

    def block(n):
        cols = slice(n * MXU_N, (n + 1) * MXU_N)

        def run():
            p_ref[:, cols] = jnp.dot(xb(), w_ref[:, cols], preferred_element_type=F32).astype(BF16)
        return run

    def gate_logits():
        z_ref[...] = jnp.dot(xb(), w_ref[:, PROJ_MAIN:], preferred_element_type=F32)

    return [block(n) for n in range(PROJ_MAIN // MXU_N)] + [gate_logits]


def _gmlp_items(uv_ref, w_ref, bcol_ref, g_ref, b_ref, o_ref):
    rows = uv_ref.shape[0]

    def head(h):
        cols = slice(h * GM_HEAD_DIM, (h + 1) * GM_HEAD_DIM)
        vcols = slice(GM_WIDTH + h * GM_HEAD_DIM, GM_WIDTH + (h + 1) * GM_HEAD_DIM)

        def run():
            r = lax.broadcasted_iota(jnp.int32, (GM_CHUNK, GM_CHUNK), 0)
            c = lax.broadcasted_iota(jnp.int32, (GM_CHUNK, GM_CHUNK), 1)
            wm = jnp.where(r >= c, w_ref[h], 0.0).astype(BF16)
            v = _gelu(uv_ref[:, vcols].astype(F32))
            vn = _layer_norm(v, g_ref[:, cols], b_ref[:, cols]).astype(BF16)
            u = _gelu(uv_ref[:, cols].astype(F32))
            for ci in range(rows // GM_CHUNK):
                rs = slice(ci * GM_CHUNK, (ci + 1) * GM_CHUNK)
                s = jnp.dot(wm, vn[rs], preferred_element_type=F32) + bcol_ref[h]
                o_ref[rs, cols] = (u[rs] * s).astype(BF16)
        return run

    return [head(h) for h in range(GM_HEADS)]


def _gla_items(q_ref, k_ref, v_ref, g_ref, z_ref, ba_ref, ng_ref, o_ref,
               st_ref, qd_ref, ki_ref, kd_ref, dec_ref, raw_ref):
    rows = q_ref.shape[0]
    n_chunks = rows // GLA_CHUNK
    shift = GLA_CHUNK.bit_length() - 1
    nt = (((1,), (1,)), ((), ()))
    tn = (((0,), (0,)), ((), ()))

    def prep():
        r = lax.broadcasted_iota(jnp.int32, (rows, rows), 0)
        c = lax.broadcasted_iota(jnp.int32, (rows, rows), 1)
        ones_tril = jnp.where(jnp.logical_and((r >> shift) == (c >> shift), r >= c),
                              1.0, 0.0).astype(BF16)
        z = z_ref[...] + ba_ref[...]
        la = (jnp.minimum(z, 0.0) - jnp.log1p(jnp.exp(-jnp.abs(z)))) * (1.0 / GLA_TAU)
        la_hi = la.astype(BF16)
        la_lo = (la - la_hi.astype(F32)).astype(BF16)
        b = (jnp.dot(ones_tril, la_hi, preferred_element_type=F32)
             + jnp.dot(ones_tril, la_lo, preferred_element_type=F32))
        b3 = b.reshape(n_chunks, GLA_CHUNK, GLA_KEY_WIDTH)
        b_last = b3[:, GLA_CHUNK - 1:GLA_CHUNK, :]
        k = k_ref[...].astype(F32)
        qd_ref[...] = (q_ref[...].astype(F32) * (GLA_DK ** -0.5) * jnp.exp(b)).astype(BF16)
        ki_ref[...] = (k * jnp.exp(-b)).astype(BF16)
        kd_ref[...] = (k * jnp.exp(b_last - b3).reshape(rows, GLA_KEY_WIDTH)).astype(BF16)
        dec_ref[...] = jnp.exp(b_last).reshape(n_chunks, GLA_KEY_WIDTH)

    def chunk(ci):
        rs = slice(ci * GLA_CHUNK, (ci + 1) * GLA_CHUNK)

        def run(between=None):
            rc = lax.broadcasted_iota(jnp.int32, (GLA_CHUNK, GLA_CHUNK), 0)
            cc = lax.broadcasted_iota(jnp.int32, (GLA_CHUNK, GLA_CHUNK), 1)
            causal = rc >= cc
            q_dec = qd_ref[rs, :]
            k_inv = ki_ref[rs, :]
            k_dec = kd_ref[rs, :]
            decay = dec_ref[ci:ci + 1, :]
            head_cols = [slice(h * GLA_DK, (h + 1) * GLA_DK) for h in range(GLA_HEADS)]
            all_scores = [lax.dot_general(q_dec[:, ks], k_inv[:, ks], nt, preferred_element_type=F32)
                          for ks in head_cols]
            if between is not None:
                between()
            for h in range(GLA_HEADS):
                ks = head_cols[h]
                vs = slice(h * GLA_DV, (h + 1) * GLA_DV)
                scores = jnp.where(causal, all_scores[h], 0.0).astype(BF16)
                vh = v_ref[rs, vs]
                state_t = st_ref[h]
                o = (jnp.dot(scores, vh, preferred_element_type=F32)
                     + lax.dot_general(q_dec[:, ks], state_t.astype(BF16), nt,
                                       preferred_element_type=F32))
                st_ref[h] = (state_t * decay[:, ks]
                             + lax.dot_general(vh, k_dec[:, ks], tn, preferred_element_type=F32))
                raw_ref[rs, vs] = o
        return run

    def finish():
        for h in range(GLA_HEADS):
            vs = slice(h * GLA_DV, (h + 1) * GLA_DV)
            o = raw_ref[:, vs]
            o = o * lax.rsqrt(jnp.mean(o * o, axis=-1, keepdims=True) + RMS_EPS) * ng_ref[:, vs]
            gate = g_ref[:, vs].astype(F32)
            o_ref[:, vs] = (o * (gate * jax.nn.sigmoid(gate))).astype(BF16)

    return prep, [chunk(ci) for ci in range(n_chunks)], finish


def _mixer_kernel(xa_ref, xc_ref, win_ref, wout_ref, lng_ref, lnb_ref,
                  gw_ref, gbcol_ref, gg_ref, gb_ref, ba_ref, ng_ref, o_ref,
                  proj0, proj1, z0, z1, ab0, ab1, st_ref, *gla_scratch, tiles_per_seq):
    t = pl.program_id(0)
    proj_scr, z_scr, ab_scr = (proj0, proj1), (z0, z1), (ab0, ab1)

    @pl.when(t == 0)
    def _():
        for ref in (proj0, proj1, z0, z1, ab0, ab1, st_ref):
            ref[...] = jnp.zeros(ref.shape, ref.dtype)

    @pl.when(lax.rem(t + tiles_per_seq - 1, tiles_per_seq) == 0)
    def _():
        st_ref[...] = jnp.zeros(st_ref.shape, F32)

    q0 = 2 * GM_WIDTH
    k0 = q0 + GLA_KEY_WIDTH
    v0 = k0 + GLA_KEY_WIDTH
    g0 = v0 + GLA_WIDTH

    def stages(proj_a, z_a, proj_b, z_b, ab_b, ab_c):
        inproj = _inproj_items(xa_ref, win_ref, proj_a, z_a)
        gla_prep, gla_chunks, gla_finish = _gla_items(
            proj_b.at[:, q0:k0], proj_b.at[:, k0:v0], proj_b.at[:, v0:g0],
            proj_b.at[:, g0:PROJ_MAIN], z_b, ba_ref, ng_ref, ab_b.at[:, GM_WIDTH:],
            st_ref, *gla_scratch)
        gmlp = _gmlp_items(proj_b.at[:, :q0], gw_ref, gbcol_ref, gg_ref, gb_ref,
                           ab_b.at[:, :GM_WIDTH])
        outproj, outproj_finish = _outproj_items(ab_c.at[:, :GM_WIDTH], ab_c.at[:, GM_WIDTH:],
                                                 wout_ref, xc_ref, lng_ref, lnb_ref, o_ref)
        fillers = inproj + outproj
        head, tail = MIXER_HEAD_FILLERS, len(fillers) - MIXER_TAIL_FILLERS
        for run in fillers[:head // 2]:
            run()
        gla_prep()
        for run in fillers[head // 2:head]:
            run()
        done = head
        for i, run_chunk in enumerate(gla_chunks):
            target = head + (i + 1) * (tail - head) // len(gla_chunks)
            mid, done = fillers[done:target], target
            run_chunk(between=lambda mid=mid: [run() for run in mid])
        late = fillers[tail:] + [gla_finish]
        for i, run_head in enumerate(gmlp):
            for run in late[i * len(late) // len(gmlp):(i + 1) * len(late) // len(gmlp)]:
                run()
            run_head()
        outproj_finish()

    for parity in range(2):
        @pl.when(t % 2 == parity)
        def _(a=parity, b=1 - parity):
            stages(proj_scr[a], z_scr[a], proj_scr[b], z_scr[b], ab_scr[b], ab_scr[a])


def _mixer_call(x, w_ext, w_out, ln_g, ln_b, gm_ws, gm_bcol, gm_ln_g, gm_ln_b, ba, norm_g, seq):
    t, d = x.shape
    n_tiles = t // ROW_TILE
    kw, vw = GLA_KEY_WIDTH, GLA_WIDTH
    const2 = lambda i: (0, 0)
    const3 = lambda i: (0, 0, 0)
    resident = pl.Buffered(1)
    tile_c = lambda i: (jnp.maximum(i - 2, 0), 0)
    return pl.pallas_call(
        functools.partial(_mixer_kernel, tiles_per_seq=seq // ROW_TILE),
        grid=(n_tiles + 2,),
        in_specs=[pl.BlockSpec((ROW_TILE, d), lambda i: (jnp.minimum(i, n_tiles - 1), 0)),
                  pl.BlockSpec((ROW_TILE, d), tile_c),
                  pl.BlockSpec(w_ext.shape, const2, pipeline_mode=resident),
                  pl.BlockSpec(w_out.shape, const2, pipeline_mode=resident),
                  pl.BlockSpec((1, d), const2),
                  pl.BlockSpec((1, d), const2),
                  pl.BlockSpec((GM_HEADS, GM_CHUNK, GM_CHUNK), const3),
                  pl.BlockSpec((GM_HEADS, GM_CHUNK, 1), const3),
                  pl.BlockSpec((1, GM_WIDTH), const2),
                  pl.BlockSpec((1, GM_WIDTH), const2),
                  pl.BlockSpec((1, kw), const2),
                  pl.BlockSpec((1, vw), const2)],
        out_specs=pl.BlockSpec((ROW_TILE, d), tile_c),
        out_shape=jax.ShapeDtypeStruct((t, d), F32),
        scratch_shapes=[pltpu.VMEM((ROW_TILE, PROJ_MAIN), BF16),
                        pltpu.VMEM((ROW_TILE, PROJ_MAIN), BF16),
                        pltpu.VMEM((ROW_TILE, kw), F32),
                        pltpu.VMEM((ROW_TILE, kw), F32),
                        pltpu.VMEM((ROW_TILE, GM_WIDTH + vw), BF16),
                        pltpu.VMEM((ROW_TILE, GM_WIDTH + vw), BF16),
                        pltpu.VMEM((GLA_HEADS, GLA_DV, GLA_DK), F32),
                        pltpu.VMEM((ROW_TILE, kw), BF16),
                        pltpu.VMEM((ROW_TILE, kw), BF16),
                        pltpu.VMEM((ROW_TILE, kw), BF16),
                        pltpu.VMEM((ROW_TILE // GLA_CHUNK, kw), F32),
                        pltpu.VMEM((ROW_TILE, vw), F32)],
        compiler_params=_params("arbitrary"),
        name="mixer",
    )(x, x, w_ext, w_out, ln_g, ln_b, gm_ws, gm_bcol, gm_ln_g, gm_ln_b, ba, norm_g)


def _outproj_items(a_ref, b_ref, w_ref, x_ref, g_ref, be_ref, o_ref):
    def block(n):
        cols = slice(n * MXU_N, (n + 1) * MXU_N)

        def run():
            h = (jnp.dot(a_ref[...], w_ref[:GM_WIDTH, cols], preferred_element_type=F32)
                 + jnp.dot(b_ref[...], w_ref[GM_WIDTH:, cols], preferred_element_type=F32))
            o_ref[:, cols] = DEEPNORM_ALPHA * x_ref[:, cols] + h
        return run

    def finish():
        o_ref[...] = _layer_norm(o_ref[...], g_ref[...], be_ref[...])

    return [block(n) for n in range(o_ref.shape[1] // MXU_N)], finish


def _swiglu_ln_kernel(x_ref, wg_ref, wu_ref, wd_ref, g_ref, b_ref, o_ref, hid_ref):
    xb = x_ref[...].astype(BF16)
    for n in range(wg_ref.shape[1] // DENSE_FF_TILE):
        cols = slice(n * DENSE_FF_TILE, (n + 1) * DENSE_FF_TILE)
        gate = jnp.dot(xb, wg_ref[:, cols], preferred_element_type=F32)
        up = jnp.dot(xb, wu_ref[:, cols], preferred_element_type=F32)
        hid_ref[:, cols] = (gate * jax.nn.sigmoid(gate) * up).astype(BF16)
    f = jnp.dot(hid_ref[...], wd_ref[...], preferred_element_type=F32)
    o_ref[...] = _layer_norm(DEEPNORM_ALPHA * x_ref[...] + f, g_ref[...], b_ref[...])


def _swiglu_ln(x, wg, wu, wd, g, b):
    rows, d = x.shape
    ff = wg.shape[1]
    resident = pl.Buffered(1)
    return pl.pallas_call(
        _swiglu_ln_kernel,
        grid=(rows // FFN_ROW_TILE,),
        in_specs=[pl.BlockSpec((FFN_ROW_TILE, d), lambda i: (i, 0)),
                  pl.BlockSpec((d, ff), lambda i: (0, 0), pipeline_mode=resident),
                  pl.BlockSpec((d, ff), lambda i: (0, 0), pipeline_mode=resident),
                  pl.BlockSpec((ff, d), lambda i: (0, 0), pipeline_mode=resident),
                  pl.BlockSpec((1, d), lambda i: (0, 0)),
                  pl.BlockSpec((1, d), lambda i: (0, 0))],
        out_specs=pl.BlockSpec((FFN_ROW_TILE, d), lambda i: (i, 0)),
        out_shape=jax.ShapeDtypeStruct((rows, d), F32),
        scratch_shapes=[pltpu.VMEM((FFN_ROW_TILE, ff), BF16)],
        compiler_params=_params("parallel"),
        name="swiglu_ln",
    )(x, wg, wu, wd, g, b)


def _split_bf16(v):
    hi = v.astype(BF16)
    return hi, (v - hi.astype(F32)).astype(BF16)


def _router_kernel(x_ref, wr_ref, ri_ref, rw_ref, cnt_ref):
    tb = x_ref.shape[0]
    x_hi, x_lo = _split_bf16(x_ref[...])
    w_hi, w_lo = _split_bf16(wr_ref[...])
    nt = (((1,), (1,)), ((), ()))
    logits = (lax.dot_general(w_hi, x_hi, nt, preferred_element_type=F32)
              + lax.dot_general(w_hi, x_lo, nt, preferred_element_type=F32)
              + lax.dot_general(w_lo, x_hi, nt, preferred_element_type=F32))
    e_iota = lax.broadcasted_iota(jnp.int32, (N_EXPERTS, tb), 0)
    m0 = jnp.max(logits, axis=0, keepdims=True)
    i0 = jnp.min(jnp.where(logits == m0, e_iota, N_EXPERTS), axis=0, keepdims=True)
    rest = jnp.where(e_iota == i0, -jnp.inf, logits)
    m1 = jnp.max(rest, axis=0, keepdims=True)
    i1 = jnp.min(jnp.where(rest == m1, e_iota, N_EXPERTS), axis=0, keepdims=True)
    ex = jnp.exp(m1 - m0)
    w0 = 1.0 / (1.0 + ex)
    w1 = ex / (1.0 + ex)
    oh0 = e_iota == i0
    oh1 = e_iota == i1
    onehot = jnp.where(jnp.logical_or(oh0, oh1), 1.0, 0.0)
    tr = lax.broadcasted_iota(jnp.int32, (tb, tb), 0)
    tc = lax.broadcasted_iota(jnp.int32, (tb, tb), 1)
    before = jnp.where(tr < tc, 1.0, 0.0).astype(BF16)
    rank = jnp.dot(onehot.astype(BF16), before, preferred_element_type=F32)
    count = jnp.broadcast_to(jnp.sum(onehot, axis=1, keepdims=True), (N_EXPERTS, LANE))
    cap = jnp.ceil(count * (1.0 / GRANULE)) * GRANULE
    e_sub = lax.broadcasted_iota(jnp.int32, (N_EXPERTS, LANE), 0)
    seg = jnp.zeros((N_EXPERTS, LANE), F32)
    for e in range(N_EXPERTS - 1):
        seg = seg + jnp.where(e_sub > e, cap[e:e + 1, :], 0.0)
    row = rank + seg[:, 0:1]
    row0 = jnp.sum(jnp.where(oh0, row, 0.0), axis=0, keepdims=True)
    row1 = jnp.sum(jnp.where(oh1, row, 0.0), axis=0, keepdims=True)
    cnt_ref[...] = count
    ri_ref[...] = jnp.zeros(ri_ref.shape, jnp.int32)
    ri_ref[0:1, :] = i0
    ri_ref[1:2, :] = i1
    ri_ref[2:3, :] = row0.astype(jnp.int32)
    ri_ref[3:4, :] = row1.astype(jnp.int32)
    rw_ref[...] = jnp.zeros(rw_ref.shape, F32)
    rw_ref[0:1, :] = w0
    rw_ref[1:2, :] = w1
    return x_hi


def _route_dispatch_kernel(x_ref, wr_ref, ri_ref, rw_ref, cnt_ref, o_ref):
    xb = _router_kernel(x_ref, wr_ref, ri_ref, rw_ref, cnt_ref)
    rows, tb = o_ref.shape[0], x_ref.shape[0]
    r = lax.broadcasted_iota(jnp.int32, (rows, tb), 0)
    sel = jnp.logical_or(r == ri_ref[2:3, :], r == ri_ref[3:4, :])
    perm = jnp.where(sel, 1.0, 0.0).astype(BF16)
    o_ref[...] = jnp.dot(perm, xb, preferred_element_type=F32).astype(BF16)


def _route_dispatch(x, wr_t):
    t, d = x.shape
    nb = t // MOE_BLOCK
    return pl.pallas_call(
        _route_dispatch_kernel,
        grid=(nb,),
        in_specs=[pl.BlockSpec((MOE_BLOCK, d), lambda i: (i, 0)),
                  pl.BlockSpec((N_EXPERTS, d), lambda i: (0, 0))],
        out_specs=[pl.BlockSpec((SUBLANE, MOE_BLOCK), lambda i: (0, i)),
                   pl.BlockSpec((SUBLANE, MOE_BLOCK), lambda i: (0, i)),
                   pl.BlockSpec((None, N_EXPERTS, LANE), lambda i: (i, 0, 0)),
                   pl.BlockSpec((LOC_ROWS, d), lambda i: (i, 0))],
        out_shape=[jax.ShapeDtypeStruct((SUBLANE, t), jnp.int32),
                   jax.ShapeDtypeStruct((SUBLANE, t), F32),
                   jax.ShapeDtypeStruct((nb, N_EXPERTS, LANE), F32),
                   jax.ShapeDtypeStruct((nb * LOC_ROWS, d), BF16)],
        compiler_params=_params("parallel"),
        name="route_dispatch",
    )(x, wr_t)


def _experts_kernel(te_ref, nu_ref, gran_ref, x_hbm, wg_hbm, wu_hbm, wd_hbm, y_hbm,
                    xbuf, ybuf, hid_ref, wg_buf, wu_buf, wd_buf, sem_in, sem_out, sem_w):
    i = pl.program_id(0)
    n_tiles = pl.num_programs(0)
    n_used = nu_ref[0]
    n_chunks = wg_buf.shape[0]
    expert = te_ref[i]
    prev_expert = te_ref[jnp.maximum(i - 1, 0)]
    next_expert = te_ref[jnp.minimum(i + 1, n_tiles - 1)]
    first_of_expert = jnp.logical_or(i == 0, prev_expert != expert)
    refill = jnp.logical_and(i + 1 < n_used, next_expert != expert)

    def up_copies(e, n):
        cols = pl.ds(n * MOE_FF_TILE, MOE_FF_TILE)
        return (pltpu.make_async_copy(wg_hbm.at[e, :, cols], wg_buf.at[n], sem_w.at[n]),
                pltpu.make_async_copy(wu_hbm.at[e, :, cols], wu_buf.at[n], sem_w.at[n]))

    def down_copy(e):
        return pltpu.make_async_copy(wd_hbm.at[e], wd_buf, sem_w.at[n_chunks])

    def for_granules(tile, fn):
        def one(s):
            g = gran_ref[tile * GRAN_PER_TILE + s]
            fn(pl.ds(pl.multiple_of(g * GRANULE, GRANULE), GRANULE),
               pl.ds(pl.multiple_of(s * GRANULE, GRANULE), GRANULE))

        def group(p, carry):
            for u in range(GRAN_UNROLL):
                one(p * GRAN_UNROLL + u)
            return carry

        def single(s, carry):
            one(s)
            return carry

        filled = nu_ref[1 + tile]
        n_groups = filled // GRAN_UNROLL
        lax.fori_loop(0, n_groups, group, 0)
        lax.fori_loop(n_groups * GRAN_UNROLL, filled, single, 0)

    def gather(slot):
        return lambda hbm_rows, tile_rows: pltpu.make_async_copy(
            x_hbm.at[hbm_rows], xbuf.at[slot, tile_rows], sem_in.at[slot])

    def scatter(hbm_rows, tile_rows):
        return pltpu.make_async_copy(ybuf.at[tile_rows], y_hbm.at[hbm_rows], sem_out.at[0])

    @pl.when(i < n_used)
    def _():
        slot = i % 2

        @pl.when(i == 0)
        def _():
            for n in range(n_chunks):
                for cp in up_copies(expert, n):
                    cp.start()
            down_copy(expert).start()
            xbuf[...] = jnp.zeros(xbuf.shape, BF16)
            for_granules(0, lambda h, r: gather(0)(h, r).start())

        for_granules(i, lambda h, r: gather(slot)(h, r).wait())

        @pl.when(i + 1 < n_used)
        def _():
            for_granules(i + 1, lambda h, r: gather(1 - slot)(h, r).start())

        @pl.when(first_of_expert)
        def _():
            for n in range(n_chunks):
                for cp in up_copies(expert, n):
                    cp.wait()

        xb = xbuf[slot]
        for n in range(n_chunks):
            gate = jnp.dot(xb, wg_buf[n], preferred_element_type=F32)
            up = jnp.dot(xb, wu_buf[n], preferred_element_type=F32)
            hid_ref[:, n * MOE_FF_TILE:(n + 1) * MOE_FF_TILE] = (
                gate * jax.nn.sigmoid(gate) * up).astype(BF16)

        @pl.when(refill)
        def _():
            for n in range(n_chunks):
                for cp in up_copies(next_expert, n):
                    cp.start()

        @pl.when(first_of_expert)
        def _():
            down_copy(expert).wait()

        @pl.when(i > 0)
        def _():
            for_granules(i - 1, lambda h, r: scatter(h, r).wait())

        ybuf[...] = jnp.dot(hid_ref[...], wd_buf[...], preferred_element_type=F32).astype(BF16)

        @pl.when(refill)
        def _():
            down_copy(next_expert).start()

        for_granules(i, lambda h, r: scatter(h, r).start())

        @pl.when(i == n_used - 1)
        def _():
            for_granules(i, lambda h, r: scatter(h, r).wait())


def _experts(tile_expert, used, gran, x_loc, wg, wu, wd):
    rows, d = x_loc.shape
    ff = wg.shape[2]
    n_tiles = tile_expert.shape[0]
    nj = ff // MOE_FF_TILE
    hbm = pl.BlockSpec(memory_space=pl.ANY)
    return pl.pallas_call(
        _experts_kernel,
        grid_spec=pltpu.PrefetchScalarGridSpec(
            num_scalar_prefetch=3,
            grid=(n_tiles,),
            in_specs=[hbm, hbm, hbm, hbm],
            out_specs=hbm,
            scratch_shapes=[pltpu.VMEM((2, FFN_ROW_TILE, d), BF16),
                            pltpu.VMEM((FFN_ROW_TILE, d), BF16),
                            pltpu.VMEM((FFN_ROW_TILE, ff), BF16),
                            pltpu.VMEM((nj, d, MOE_FF_TILE), BF16),
                            pltpu.VMEM((nj, d, MOE_FF_TILE), BF16),
                            pltpu.VMEM((ff, d), BF16),
                            pltpu.SemaphoreType.DMA((2,)),
                            pltpu.SemaphoreType.DMA((1,)),
                            pltpu.SemaphoreType.DMA((nj + 1,))]),
        out_shape=jax.ShapeDtypeStruct((rows, d), BF16),
        input_output_aliases={3: 0},
        compiler_params=_params("arbitrary"),
        name="swiglu_experts",
    )(tile_expert, used, gran, x_loc, wg, wu, wd)


def _combine_ln_kernel(y_ref, row_ref, w_ref, x_ref, g_ref, b_ref, o_ref):
    tb, rows = x_ref.shape[0], y_ref.shape[0]
    for k in range(tb // COMBINE_SUB):
        ts = slice(k * COMBINE_SUB, (k + 1) * COMBINE_SUB)
        c = lax.broadcasted_iota(jnp.int32, (COMBINE_SUB, rows), 1)
        q = (jnp.where(c == row_ref[ts, 0:1], w_ref[ts, 0:1], 0.0)
             + jnp.where(c == row_ref[ts, 1:2], w_ref[ts, 1:2], 0.0))
        f = jnp.dot(q.astype(BF16), y_ref[...], preferred_element_type=F32)
        o_ref[ts, :] = _layer_norm(DEEPNORM_ALPHA * x_ref[ts, :] + f, g_ref[...], b_ref[...])


def _combine_ln(y_loc, row_col, w_col, x, g, b):
    t, d = x.shape
    return pl.pallas_call(
        _combine_ln_kernel,
        grid=(t // MOE_BLOCK,),
        in_specs=[pl.BlockSpec((LOC_ROWS, d), lambda i: (i, 0)),
                  pl.BlockSpec((MOE_BLOCK, TOP_K), lambda i: (i, 0)),
                  pl.BlockSpec((MOE_BLOCK, TOP_K), lambda i: (i, 0)),
                  pl.BlockSpec((MOE_BLOCK, d), lambda i: (i, 0)),
                  pl.BlockSpec((1, d), lambda i: (0, 0)),
                  pl.BlockSpec((1, d), lambda i: (0, 0))],
        out_specs=pl.BlockSpec((MOE_BLOCK, d), lambda i: (i, 0)),
        out_shape=jax.ShapeDtypeStruct((t, d), F32),
        compiler_params=_params("parallel"),
        name="combine_ln",
    )(y_loc, row_col, w_col, x, g, b)


def _mixer(x, batch, seq, w_in, gm_ws, gm_bs, gm_ln_g, gm_ln_b, gla_wa2, gla_ba, gla_norm_g,
           w_out, ln_g, ln_b):
    w_gate = _fold_gate(w_in[:, PROJ_MAIN:], gla_wa2)
    w_ext = jnp.concatenate([w_in[:, :PROJ_MAIN], w_gate], axis=1).astype(BF16)
    del batch
    return _mixer_call(x, w_ext, w_out.astype(BF16), ln_g.reshape(1, -1), ln_b.reshape(1, -1),
                       gm_ws, gm_bs.reshape(GM_HEADS, GM_CHUNK, 1),
                       gm_ln_g.reshape(1, -1), gm_ln_b.reshape(1, -1),
                       gla_ba.reshape(1, -1), gla_norm_g.reshape(1, -1), seq)


def _granule_table(counts):
    nb = counts.shape[0]
    n_tiles = (nb * (TOP_K * MOE_BLOCK + N_EXPERTS * (GRANULE - 1))) // FFN_ROW_TILE + N_EXPERTS
    gran_per_block = LOC_ROWS // GRANULE
    g = (counts + GRANULE - 1) // GRANULE
    seg_start = jnp.cumsum(g, axis=1) - g
    cum_incl = jnp.cumsum(g, axis=0)
    cum_excl = cum_incl - g
    total = cum_incl[-1]
    tiles_per_expert = (total + GRAN_PER_TILE - 1) // GRAN_PER_TILE
    tile_end = jnp.cumsum(tiles_per_expert)
    n_used = tile_end[-1:]
    tile = jnp.arange(n_tiles, dtype=jnp.int32)
    tile_expert = jnp.minimum(jnp.sum(tile[:, None] >= tile_end[None, :], axis=1), N_EXPERTS - 1)
    pick = (tile_expert[:, None] == jnp.arange(N_EXPERTS, dtype=jnp.int32)[None, :]).astype(jnp.int32)
    per_tile = lambda tab: jnp.sum(pick[:, :, None] * tab.T[None, :, :], axis=1)
    first_tile = jnp.sum(pick * (tile_end - tiles_per_expert)[None, :], axis=1)
    vg = ((tile - first_tile)[:, None] * GRAN_PER_TILE
          + jnp.arange(GRAN_PER_TILE, dtype=jnp.int32)[None, :])[:, :, None]
    lo = per_tile(cum_excl)[:, None, :]
    hi = per_tile(cum_incl)[:, None, :]
    base = (jnp.arange(nb, dtype=jnp.int32)[None, :] * gran_per_block + per_tile(seg_start))[:, None, :]
    inside = jnp.logical_and(vg >= lo, vg < hi)
    phys = jnp.sum(jnp.where(inside, base + vg - lo, 0), axis=2)
    valid = jnp.logical_and(jnp.any(inside, axis=2), (tile < n_used[0])[:, None])
    gran = jnp.where(valid, phys, 0).astype(jnp.int32).reshape(-1)
    used = jnp.concatenate([n_used, jnp.sum(valid, axis=1)]).astype(jnp.int32)
    return tile_expert.astype(jnp.int32), used, gran


def _moe_ffn(x, w_router, e_gate, e_up, e_down, ln_g, ln_b):
    ri, rw, cnt, x_loc = _route_dispatch(x, w_router.T)
    tile_expert, n_used, gran = _granule_table(cnt[:, :, 0].astype(jnp.int32))
    y_loc = _experts(tile_expert, n_used, gran, x_loc,
                     e_gate.astype(BF16), e_up.astype(BF16), e_down.astype(BF16))
    return _combine_ln(y_loc, ri[2:2 + TOP_K].T, rw[:TOP_K].T, x,
                       ln_g.reshape(1, -1), ln_b.reshape(1, -1))


def kernel(x, w_in, gm_ws, gm_bs, gm_ln_g, gm_ln_b, gla_wa2, gla_ba, gla_norm_g, w_out, ln_mix_g, ln_mix_b, ffn_w_gate, ffn_w_up, ffn_w_down, router_w, exp_w_gate, exp_w_up, exp_w_down, ln_ffn_g, ln_ffn_b):
    batch, seq, d = x.shape
    h = x.reshape(batch * seq, d)
    for layer in range(DEPTH):
        h = _mixer(h, batch, seq, w_in[layer], gm_ws[layer], gm_bs[layer], gm_ln_g[layer],
                   gm_ln_b[layer], gla_wa2[layer], gla_ba[layer], gla_norm_g[layer], w_out[layer],
                   ln_mix_g[layer], ln_mix_b[layer])
        i = layer // 2
        if layer % 2 == 0:
            h = _swiglu_ln(h, ffn_w_gate[i].astype(BF16), ffn_w_up[i].astype(BF16),
                           ffn_w_down[i].astype(BF16), ln_ffn_g[layer].reshape(1, -1),
                           ln_ffn_b[layer].reshape(1, -1))
        else:
            h = _moe_ffn(h, router_w[i], exp_w_gate[i], exp_w_up[i], exp_w_down[i],
                         ln_ffn_g[layer], ln_ffn_b[layer])
    return h.reshape(batch, seq, d)
```

```python
import functools

import jax
import jax.numpy as jnp
from jax import lax
from jax.experimental import pallas as pl
from jax.experimental.pallas import tpu as pltpu

F32 = jnp.float32
BF16 = jnp.bfloat16

D_MODEL = 1024
DEPTH = 2
GM_HEADS = 4
GM_WIDTH = 512
GM_HEAD_DIM = 128
GM_CHUNK = 128
GLA_HEADS = 4
GLA_WIDTH = 512
GLA_DV = 128
GLA_KEY_WIDTH = 256
GLA_DK = 64
GLA_GATE_RANK = 16
GLA_TAU = 16.0
GLA_CHUNK = 64
PROJ_MAIN = 2 * GM_WIDTH + 2 * GLA_KEY_WIDTH + 2 * GLA_WIDTH
N_EXPERTS = 8
TOP_K = 2
LN_EPS = 1e-5
RMS_EPS = 1e-6
DEEPNORM_ALPHA = (2 * DEPTH) ** 0.25

V7X_VMEM_LIMIT_BYTES = 56 * 1024 * 1024
LANE = 128
SUBLANE = 8
MXU_N = 256

ROW_TILE = 512
FFN_ROW_TILE = 1024
DENSE_FF_TILE = 256
MOE_FF_TILE = 512
MOE_BLOCK = 512
GRANULE = 16
LOC_ROWS = -(-(TOP_K * MOE_BLOCK + N_EXPERTS * (GRANULE - 1)) // LANE) * LANE
COMBINE_SUB = 256
MIXER_HEAD_FILLERS = 6
MIXER_TAIL_FILLERS = 3
GRAN_UNROLL = 4
GRAN_PER_TILE = FFN_ROW_TILE // GRANULE


def _params(*sem, **kw):
    return pltpu.CompilerParams(dimension_semantics=sem, vmem_limit_bytes=V7X_VMEM_LIMIT_BYTES, **kw)


def _gelu(x):
    return 0.5 * x * (1.0 + lax.erf(x * (2.0 ** -0.5)))


def _layer_norm(y, g, b):
    mu = jnp.mean(y, axis=-1, keepdims=True)
    d = y - mu
    var = jnp.mean(d * d, axis=-1, keepdims=True)
    return d * lax.rsqrt(var + LN_EPS) * g + b


def _fold_kernel(w1_ref, w2_ref, o_ref):
    acc = jnp.zeros(o_ref.shape, F32)
    for r in range(GLA_GATE_RANK):
        acc = acc + w1_ref[:, r:r + 1] * w2_ref[r:r + 1, :]
    o_ref[...] = acc


def _fold_gate(w1, w2):
    return pl.pallas_call(
        _fold_kernel,
        out_shape=jax.ShapeDtypeStruct((w1.shape[0], w2.shape[1]), F32),
        name="fold_gate",
    )(w1, w2)


def _inproj_items(x_ref, w_ref, p_ref, z_ref):
    cache = {}

    def xb():
        if "xb" not in cache:
            cache["xb"] = x_ref[...].astype(BF16)
        return cache["xb"]

    def block(n):
        cols = slice(n * MXU_N, (n + 1) * MXU_N)

        def run():
            p_ref[:, cols] = jnp.dot(xb(), w_ref[:, cols], preferred_element_type=F32).astype(BF16)
        return run

    def gate_logits():
        z_ref[...] = jnp.dot(xb(), w_ref[:, PROJ_MAIN:], preferred_element_type=F32)

    return [block(n) for n in range(PROJ_MAIN // MXU_N)] + [gate_logits]


def _gmlp_items(uv_ref, w_ref, bcol_ref, g_ref, b_ref, o_ref):
    rows = uv_ref.shape[0]

    def head(h):
        cols = slice(h * GM_HEAD_DIM, (h + 1) * GM_HEAD_DIM)
        vcols = slice(GM_WIDTH + h * GM_HEAD_DIM, GM_WIDTH + (h + 1) * GM_HEAD_DIM)

        def run():
            r = lax.broadcasted_iota(jnp.int32, (GM_CHUNK, GM_CHUNK), 0)
            c = lax.broadcasted_iota(jnp.int32, (GM_CHUNK, GM_CHUNK), 1)
            wm = jnp.where(r >= c, w_ref[h], 0.0).astype(BF16)
            v = _gelu(uv_ref[:, vcols].astype(F32))
            vn = _layer_norm(v, g_ref[:, cols], b_ref[:, cols]).astype(BF16)
            u = _gelu(uv_ref[:, cols].astype(F32))
            for ci in range(rows // GM_CHUNK):
                rs = slice(ci * GM_CHUNK, (ci + 1) * GM_CHUNK)
                s = jnp.dot(wm, vn[rs], preferred_element_type=F32) + bcol_ref[h]
                o_ref[rs, cols] = (u[rs] * s).astype(BF16)
        return run

    return [head(h) for h in range(GM_HEADS)]


def _gla_items(q_ref, k_ref, v_ref, g_ref, z_ref, ba_ref, ng_ref, o_ref,
               st_ref, qd_ref, ki_ref, kd_ref, dec_ref, raw_ref):
    rows = q_ref.shape[0]
    n_chunks = rows // GLA_CHUNK
    shift = GLA_CHUNK.bit_length() - 1
    nt = (((1,), (1,)), ((), ()))
    tn = (((0,), (0,)), ((), ()))

    def prep():
        r = lax.broadcasted_iota(jnp.int32, (rows, rows), 0)
        c = lax.broadcasted_iota(jnp.int32, (rows, rows), 1)
        ones_tril = jnp.where(jnp.logical_and((r >> shift) == (c >> shift), r >= c),
                              1.0, 0.0).astype(BF16)
        z = z_ref[...] + ba_ref[...]
        la = (jnp.minimum(z, 0.0) - jnp.log1p(jnp.exp(-jnp.abs(z)))) * (1.0 / GLA_TAU)
        la_hi = la.astype(BF16)
        la_lo = (la - la_hi.astype(F32)).astype(BF16)
        b = (jnp.dot(ones_tril, la_hi, preferred_element_type=F32)
             + jnp.dot(ones_tril, la_lo, preferred_element_type=F32))
        b3 = b.reshape(n_chunks, GLA_CHUNK, GLA_KEY_WIDTH)
        b_last = b3[:, GLA_CHUNK - 1:GLA_CHUNK, :]
        k = k_ref[...].astype(F32)
        qd_ref[...] = (q_ref[...].astype(F32) * (GLA_DK ** -0.5) * jnp.exp(b)).astype(BF16)
        ki_ref[...] = (k * jnp.exp(-b)).astype(BF16)
        kd_ref[...] = (k * jnp.exp(b_last - b3).reshape(rows, GLA_KEY_WIDTH)).astype(BF16)
        dec_ref[...] = jnp.exp(b_last).reshape(n_chunks, GLA_KEY_WIDTH)

    def chunk(ci):
        rs = slice(ci * GLA_CHUNK, (ci + 1) * GLA_CHUNK)

        def run(between=None):
            rc = lax.broadcasted_iota(jnp.int32, (GLA_CHUNK, GLA_CHUNK), 0)
            cc = lax.broadcasted_iota(jnp.int32, (GLA_CHUNK, GLA_CHUNK), 1)
            causal = rc >= cc
            q_dec = qd_ref[rs, :]
            k_inv = ki_ref[rs, :]
            k_dec = kd_ref[rs, :]
            decay = dec_ref[ci:ci + 1, :]
            head_cols = [slice(h * GLA_DK, (h + 1) * GLA_DK) for h in range(GLA_HEADS)]
            all_scores = [lax.dot_general(q_dec[:, ks], k_inv[:, ks], nt, preferred_element_type=F32)
                          for ks in head_cols]
            if between is not None:
                between()
            for h in range(GLA_HEADS):
                ks = head_cols[h]
                vs = slice(h * GLA_DV, (h + 1) * GLA_DV)
                scores = jnp.where(causal, all_scores[h], 0.0).astype(BF16)
                vh = v_ref[rs, vs]
                state_t = st_ref[h]
                o = (jnp.dot(scores, vh, preferred_element_type=F32)
                     + lax.dot_general(q_dec[:, ks], state_t.astype(BF16), nt,
                                       preferred_element_type=F32))
                st_ref[h] = (state_t * decay[:, ks]
                             + lax.dot_general(vh, k_dec[:, ks], tn, preferred_element_type=F32))
                raw_ref[rs, vs] = o
        return run

    def finish():
        for h in range(GLA_HEADS):
            vs = slice(h * GLA_DV, (h + 1) * GLA_DV)
            o = raw_ref[:, vs]
            o = o * lax.rsqrt(jnp.mean(o * o, axis=-1, keepdims=True) + RMS_EPS) * ng_ref[:, vs]
            gate = g_ref[:, vs].astype(F32)
            o_ref[:, vs] = (o * (gate * jax.nn.sigmoid(gate))).astype(BF16)

    return prep, [chunk(ci) for ci in range(n_chunks)], finish


def _mixer_kernel(xa_ref, xc_ref, win_ref, wout_ref, lng_ref, lnb_ref,
                  gw_ref, gbcol_ref, gg_ref, gb_ref, ba_ref, ng_ref, o_ref,
                  proj0, proj1, z0, z1, ab0, ab1, st_ref, *gla_scratch, tiles_per_seq):
    t = pl.program_id(0)
    proj_scr, z_scr, ab_scr = (proj0, proj1), (z0, z1), (ab0, ab1)

    @pl.when(t == 0)
    def _():
        for ref in (proj0, proj1, z0, z1, ab0, ab1, st_ref):
            ref[...] = jnp.zeros(ref.shape, ref.dtype)

    @pl.when(lax.rem(t + tiles_per_seq - 1, tiles_per_seq) == 0)
    def _():
        st_ref[...] = jnp.zeros(st_ref.shape, F32)

    q0 = 2 * GM_WIDTH
    k0 = q0 + GLA_KEY_WIDTH
    v0 = k0 + GLA_KEY_WIDTH
    g0 = v0 + GLA_WIDTH

    def stages(proj_a, z_a, proj_b, z_b, ab_b, ab_c):
        inproj = _inproj_items(xa_ref, win_ref, proj_a, z_a)
        gla_prep, gla_chunks, gla_finish = _gla_items(
            proj_b.at[:, q0:k0], proj_b.at[:, k0:v0], proj_b.at[:, v0:g0],
            proj_b.at[:, g0:PROJ_MAIN], z_b, ba_ref, ng_ref, ab_b.at[:, GM_WIDTH:],
            st_ref, *gla_scratch)
        gmlp = _gmlp_items(proj_b.at[:, :q0], gw_ref, gbcol_ref, gg_ref, gb_ref,
                           ab_b.at[:, :GM_WIDTH])
        outproj, outproj_finish = _outproj_items(ab_c.at[:, :GM_WIDTH], ab_c.at[:, GM_WIDTH:],
                                                 wout_ref, xc_ref, lng_ref, lnb_ref, o_ref)
        fillers = inproj + outproj
        head, tail = MIXER_HEAD_FILLERS, len(fillers) - MIXER_TAIL_FILLERS
        for run in fillers[:head // 2]:
            run()
        gla_prep()
        for run in fillers[head // 2:head]:
            run()
        done = head
        for i, run_chunk in enumerate(gla_chunks):
            target = head + (i + 1) * (tail - head) // len(gla_chunks)
            mid, done = fillers[done:target], target
            run_chunk(between=lambda mid=mid: [run() for run in mid])
        late = fillers[tail:] + [gla_finish]
        for i, run_head in enumerate(gmlp):
            for run in late[i * len(late) // len(gmlp):(i + 1) * len(late) // len(gmlp)]:
                run()
            run_head()
        outproj_finish()

    for parity in range(2):
        @pl.when(t % 2 == parity)
        def _(a=parity, b=1 - parity):
            stages(proj_scr[a], z_scr[a], proj_scr[b], z_scr[b], ab_scr[b], ab_scr[a])


def _mixer_call(x, w_ext, w_out, ln_g, ln_b, gm_ws, gm_bcol, gm_ln_g, gm_ln_b, ba, norm_g, seq):
    t, d = x.shape
    n_tiles = t // ROW_TILE
    kw, vw = GLA_KEY_WIDTH, GLA_WIDTH
    const2 = lambda i: (0, 0)
    const3 = lambda i: (0, 0, 0)
    resident = pl.Buffered(1)
    tile_c = lambda i: (jnp.maximum(i - 2, 0), 0)
    return pl.pallas_call(
        functools.partial(_mixer_kernel, tiles_per_seq=seq // ROW_TILE),
        grid=(n_tiles + 2,),
        in_specs=[pl.BlockSpec((ROW_TILE, d), lambda i: (jnp.minimum(i, n_tiles - 1), 0)),
                  pl.BlockSpec((ROW_TILE, d), tile_c),
                  pl.BlockSpec(w_ext.shape, const2, pipeline_mode=resident),
                  pl.BlockSpec(w_out.shape, const2, pipeline_mode=resident),
                  pl.BlockSpec((1, d), const2),
                  pl.BlockSpec((1, d), const2),
                  pl.BlockSpec((GM_HEADS, GM_CHUNK, GM_CHUNK), const3),
                  pl.BlockSpec((GM_HEADS, GM_CHUNK, 1), const3),
                  pl.BlockSpec((1, GM_WIDTH), const2),
                  pl.BlockSpec((1, GM_WIDTH), const2),
                  pl.BlockSpec((1, kw), const2),
                  pl.BlockSpec((1, vw), const2)],
        out_specs=pl.BlockSpec((ROW_TILE, d), tile_c),
        out_shape=jax.ShapeDtypeStruct((t, d), F32),
        scratch_shapes=[pltpu.VMEM((ROW_TILE, PROJ_MAIN), BF16),
                        pltpu.VMEM((ROW_TILE, PROJ_MAIN), BF16),
                        pltpu.VMEM((ROW_TILE, kw), F32),
                        pltpu.VMEM((ROW_TILE, kw), F32),
                        pltpu.VMEM((ROW_TILE, GM_WIDTH + vw), BF16),
                        pltpu.VMEM((ROW_TILE, GM_WIDTH + vw), BF16),
                        pltpu.VMEM((GLA_HEADS, GLA_DV, GLA_DK), F32),
                        pltpu.VMEM((ROW_TILE, kw), BF16),
                        pltpu.VMEM((ROW_TILE, kw), BF16),
                        pltpu.VMEM((ROW_TILE, kw), BF16),
                        pltpu.VMEM((ROW_TILE // GLA_CHUNK, kw), F32),
                        pltpu.VMEM((ROW_TILE, vw), F32)],
        compiler_params=_params("arbitrary"),
        name="mixer",
    )(x, x, w_ext, w_out, ln_g, ln_b, gm_ws, gm_bcol, gm_ln_g, gm_ln_b, ba, norm_g)


def _outproj_items(a_ref, b_ref, w_ref, x_ref, g_ref, be_ref, o_ref):
    def block(n):
        cols = slice(n * MXU_N, (n + 1) * MXU_N)

        def run():
            h = (jnp.dot(a_ref[...], w_ref[:GM_WIDTH, cols], preferred_element_type=F32)
                 + jnp.dot(b_ref[...], w_ref[GM_WIDTH:, cols], preferred_element_type=F32))
            o_ref[:, cols] = DEEPNORM_ALPHA * x_ref[:, cols] + h
        return run

    def finish():
        o_ref[...] = _layer_norm(o_ref[...], g_ref[...], be_ref[...])

    return [block(n) for n in range(o_ref.shape[1] // MXU_N)], finish


def _swiglu_ln_kernel(x_ref, wg_ref, wu_ref, wd_ref, g_ref, b_ref, o_ref, hid_ref):
    xb = x_ref[...].astype(BF16)
    for n in range(wg_ref.shape[1] // DENSE_FF_TILE):
        cols = slice(n * DENSE_FF_TILE, (n + 1) * DENSE_FF_TILE)
        gate = jnp.dot(xb, wg_ref[:, cols], preferred_element_type=F32)
        up = jnp.dot(xb, wu_ref[:, cols], preferred_element_type=F32)
        hid_ref[:, cols] = (gate * jax.nn.sigmoid(gate) * up).astype(BF16)
    f = jnp.dot(hid_ref[...], wd_ref[...], preferred_element_type=F32)
    o_ref[...] = _layer_norm(DEEPNORM_ALPHA * x_ref[...] + f, g_ref[...], b_ref[...])


def _swiglu_ln(x, wg, wu, wd, g, b):
    rows, d = x.shape
    ff = wg.shape[1]
    resident = pl.Buffered(1)
    return pl.pallas_call(
        _swiglu_ln_kernel,
        grid=(rows // FFN_ROW_TILE,),
        in_specs=[pl.BlockSpec((FFN_ROW_TILE, d), lambda i: (i, 0)),
                  pl.BlockSpec((d, ff), lambda i: (0, 0), pipeline_mode=resident),
                  pl.BlockSpec((d, ff), lambda i: (0, 0), pipeline_mode=resident),
                  pl.BlockSpec((ff, d), lambda i: (0, 0), pipeline_mode=resident),
                  pl.BlockSpec((1, d), lambda i: (0, 0)),
                  pl.BlockSpec((1, d), lambda i: (0, 0))],
        out_specs=pl.BlockSpec((FFN_ROW_TILE, d), lambda i: (i, 0)),
        out_shape=jax.ShapeDtypeStruct((rows, d), F32),
        scratch_shapes=[pltpu.VMEM((FFN_ROW_TILE, ff), BF16)],
        compiler_params=_params("parallel"),
        name="swiglu_ln",
    )(x, wg, wu, wd, g, b)


def _split_bf16(v):
    hi = v.astype(BF16)
    return hi, (v - hi.astype(F32)).astype(BF16)


def _router_kernel(x_ref, wr_ref, ri_ref, rw_ref, cnt_ref):
    tb = x_ref.shape[0]
    x_hi, x_lo = _split_bf16(x_ref[...])
    w_hi, w_lo = _split_bf16(wr_ref[...])
    nt = (((1,), (1,)), ((), ()))
    logits = (lax.dot_general(w_hi, x_hi, nt, preferred_element_type=F32)
              + lax.dot_general(w_hi, x_lo, nt, preferred_element_type=F32)
              + lax.dot_general(w_lo, x_hi, nt, preferred_element_type=F32))
    e_iota = lax.broadcasted_iota(jnp.int32, (N_EXPERTS, tb), 0)
    m0 = jnp.max(logits, axis=0, keepdims=True)
    i0 = jnp.min(jnp.where(logits == m0, e_iota, N_EXPERTS), axis=0, keepdims=True)
    rest = jnp.where(e_iota == i0, -jnp.inf, logits)
    m1 = jnp.max(rest, axis=0, keepdims=True)
    i1 = jnp.min(jnp.where(rest == m1, e_iota, N_EXPERTS), axis=0, keepdims=True)
    ex = jnp.exp(m1 - m0)
    w0 = 1.0 / (1.0 + ex)
    w1 = ex / (1.0 + ex)
    oh0 = e_iota == i0
    oh1 = e_iota == i1
    onehot = jnp.where(jnp.logical_or(oh0, oh1), 1.0, 0.0)
    tr = lax.broadcasted_iota(jnp.int32, (tb, tb), 0)
    tc = lax.broadcasted_iota(jnp.int32, (tb, tb), 1)
    before = jnp.where(tr < tc, 1.0, 0.0).astype(BF16)
    rank = jnp.dot(onehot.astype(BF16), before, preferred_element_type=F32)
    count = jnp.broadcast_to(jnp.sum(onehot, axis=1, keepdims=True), (N_EXPERTS, LANE))
    cap = jnp.ceil(count * (1.0 / GRANULE)) * GRANULE
    e_sub = lax.broadcasted_iota(jnp.int32, (N_EXPERTS, LANE), 0)
    seg = jnp.zeros((N_EXPERTS, LANE), F32)
    for e in range(N_EXPERTS - 1):
        seg = seg + jnp.where(e_sub > e, cap[e:e + 1, :], 0.0)
    row = rank + seg[:, 0:1]
    row0 = jnp.sum(jnp.where(oh0, row, 0.0), axis=0, keepdims=True)
    row1 = jnp.sum(jnp.where(oh1, row, 0.0), axis=0, keepdims=True)
    cnt_ref[...] = count
    ri_ref[...] = jnp.zeros(ri_ref.shape, jnp.int32)
    ri_ref[0:1, :] = i0
    ri_ref[1:2, :] = i1
    ri_ref[2:3, :] = row0.astype(jnp.int32)
    ri_ref[3:4, :] = row1.astype(jnp.int32)
    rw_ref[...] = jnp.zeros(rw_ref.shape, F32)
    rw_ref[0:1, :] = w0
    rw_ref[1:2, :] = w1
    return x_hi


def _route_dispatch_kernel(x_ref, wr_ref, ri_ref, rw_ref, cnt_ref, o_ref):
    xb = _router_kernel(x_ref, wr_ref, ri_ref, rw_ref, cnt_ref)
    rows, tb = o_ref.shape[0], x_ref.shape[0]
    r = lax.broadcasted_iota(jnp.int32, (rows, tb), 0)
    sel = jnp.logical_or(r == ri_ref[2:3, :], r == ri_ref[3:4, :])
    perm = jnp.where(sel, 1.0, 0.0).astype(BF16)
    o_ref[...] = jnp.dot(perm, xb, preferred_element_type=F32).astype(BF16)


def _route_dispatch(x, wr_t):
    t, d = x.shape
    nb = t // MOE_BLOCK
    return pl.pallas_call(
        _route_dispatch_kernel,
        grid=(nb,),
        in_specs=[pl.BlockSpec((MOE_BLOCK, d), lambda i: (i, 0)),
                  pl.BlockSpec((N_EXPERTS, d), lambda i: (0, 0))],
        out_specs=[pl.BlockSpec((SUBLANE, MOE_BLOCK), lambda i: (0, i)),
                   pl.BlockSpec((SUBLANE, MOE_BLOCK), lambda i: (0, i)),
                   pl.BlockSpec((None, N_EXPERTS, LANE), lambda i: (i, 0, 0)),
                   pl.BlockSpec((LOC_ROWS, d), lambda i: (i, 0))],
        out_shape=[jax.ShapeDtypeStruct((SUBLANE, t), jnp.int32),
                   jax.ShapeDtypeStruct((SUBLANE, t), F32),
                   jax.ShapeDtypeStruct((nb, N_EXPERTS, LANE), F32),
                   jax.ShapeDtypeStruct((nb * LOC_ROWS, d), BF16)],
        compiler_params=_params("parallel"),
        name="route_dispatch",
    )(x, wr_t)


def _experts_kernel(te_ref, nu_ref, gran_ref, x_hbm, wg_hbm, wu_hbm, wd_hbm, y_hbm,
                    xbuf, ybuf, hid_ref, wg_buf, wu_buf, wd_buf, sem_in, sem_out, sem_w):
    i = pl.program_id(0)
    n_tiles = pl.num_programs(0)
    n_used = nu_ref[0]
    n_chunks = wg_buf.shape[0]
    expert = te_ref[i]
    prev_expert = te_ref[jnp.maximum(i - 1, 0)]
    next_expert = te_ref[jnp.minimum(i + 1, n_tiles - 1)]
    first_of_expert = jnp.logical_or(i == 0, prev_expert != expert)
    refill = jnp.logical_and(i + 1 < n_used, next_expert != expert)

    def up_copies(e, n):
        cols = pl.ds(n * MOE_FF_TILE, MOE_FF_TILE)
        return (pltpu.make_async_copy(wg_hbm.at[e, :, cols], wg_buf.at[n], sem_w.at[n]),
                pltpu.make_async_copy(wu_hbm.at[e, :, cols], wu_buf.at[n], sem_w.at[n]))

    def down_copy(e):
        return pltpu.make_async_copy(wd_hbm.at[e], wd_buf, sem_w.at[n_chunks])

    def for_granules(tile, fn):
        def one(s):
            g = gran_ref[tile * GRAN_PER_TILE + s]
            fn(pl.ds(pl.multiple_of(g * GRANULE, GRANULE), GRANULE),
               pl.ds(pl.multiple_of(s * GRANULE, GRANULE), GRANULE))

        def group(p, carry):
            for u in range(GRAN_UNROLL):
                one(p * GRAN_UNROLL + u)
            return carry

        def single(s, carry):
            one(s)
            return carry

        filled = nu_ref[1 + tile]
        n_groups = filled // GRAN_UNROLL
        lax.fori_loop(0, n_groups, group, 0)
        lax.fori_loop(n_groups * GRAN_UNROLL, filled, single, 0)

    def gather(slot):
        return lambda hbm_rows, tile_rows: pltpu.make_async_copy(
            x_hbm.at[hbm_rows], xbuf.at[slot, tile_rows], sem_in.at[slot])

    def scatter(slot):
        return lambda hbm_rows, tile_rows: pltpu.make_async_copy(
            ybuf.at[slot, tile_rows], y_hbm.at[hbm_rows], sem_out.at[slot])

    @pl.when(i < n_used)
    def _():
        slot = i % 2

        @pl.when(i == 0)
        def _():
            for n in range(n_chunks):
                for cp in up_copies(expert, n):
                    cp.start()
            down_copy(expert).start()
            xbuf[...] = jnp.zeros(xbuf.shape, BF16)
            for_granules(0, lambda h, r: gather(0)(h, r).start())

        for_granules(i, lambda h, r: gather(slot)(h, r).wait())

        @pl.when(i + 1 < n_used)
        def _():
            for_granules(i + 1, lambda h, r: gather(1 - slot)(h, r).start())

        @pl.when(i >= 2)
        def _():
            for_granules(i - 2, lambda h, r: scatter(slot)(h, r).wait())

        @pl.when(first_of_expert)
        def _():
            for n in range(n_chunks):
                for cp in up_copies(expert, n):
                    cp.wait()

        xb = xbuf[slot]
        for n in range(n_chunks):
            gate = jnp.dot(xb, wg_buf[n], preferred_element_type=F32)
            up = jnp.dot(xb, wu_buf[n], preferred_element_type=F32)
            hid_ref[:, n * MOE_FF_TILE:(n + 1) * MOE_FF_TILE] = (
                gate * jax.nn.sigmoid(gate) * up).astype(BF16)

        @pl.when(refill)
        def _():
            for n in range(n_chunks):
                for cp in up_copies(next_expert, n):
                    cp.start()

        @pl.when(first_of_expert)
        def _():
            down_copy(expert).wait()

        ybuf[slot] = jnp.dot(hid_ref[...], wd_buf[...], preferred_element_type=F32).astype(BF16)

        @pl.when(refill)
        def _():
            down_copy(next_expert).start()

        for_granules(i, lambda h, r: scatter(slot)(h, r).start())

        @pl.when(i == n_used - 1)
        def _():
            @pl.when(i > 0)
            def _():
                for_granules(i - 1, lambda h, r: scatter(1 - slot)(h, r).wait())

            for_granules(i, lambda h, r: scatter(slot)(h, r).wait())


def _experts(tile_expert, used, gran, x_loc, wg, wu, wd):
    rows, d = x_loc.shape
    ff = wg.shape[2]
    n_tiles = tile_expert.shape[0]
    nj = ff // MOE_FF_TILE
    hbm = pl.BlockSpec(memory_space=pl.ANY)
    return pl.pallas_call(
        _experts_kernel,
        grid_spec=pltpu.PrefetchScalarGridSpec(
            num_scalar_prefetch=3,
            grid=(n_tiles,),
            in_specs=[hbm, hbm, hbm, hbm],
            out_specs=hbm,
            scratch_shapes=[pltpu.VMEM((2, FFN_ROW_TILE, d), BF16),
                            pltpu.VMEM((2, FFN_ROW_TILE, d), BF16),
                            pltpu.VMEM((FFN_ROW_TILE, ff), BF16),
                            pltpu.VMEM((nj, d, MOE_FF_TILE), BF16),
                            pltpu.VMEM((nj, d, MOE_FF_TILE), BF16),
                            pltpu.VMEM((ff, d), BF16),
                            pltpu.SemaphoreType.DMA((2,)),
                            pltpu.SemaphoreType.DMA((2,)),
                            pltpu.SemaphoreType.DMA((nj + 1,))]),
        out_shape=jax.ShapeDtypeStruct((rows, d), BF16),
        input_output_aliases={3: 0},
        compiler_params=_params("arbitrary"),
        name="swiglu_experts",
    )(tile_expert, used, gran, x_loc, wg, wu, wd)


def _combine_ln_kernel(y_ref, row_ref, w_ref, x_ref, g_ref, b_ref, o_ref):
    tb, rows = x_ref.shape[0], y_ref.shape[0]
    for k in range(tb // COMBINE_SUB):
        ts = slice(k * COMBINE_SUB, (k + 1) * COMBINE_SUB)
        c = lax.broadcasted_iota(jnp.int32, (COMBINE_SUB, rows), 1)
        q = (jnp.where(c == row_ref[ts, 0:1], w_ref[ts, 0:1], 0.0)
             + jnp.where(c == row_ref[ts, 1:2], w_ref[ts, 1:2], 0.0))
        f = jnp.dot(q.astype(BF16), y_ref[...], preferred_element_type=F32)
        o_ref[ts, :] = _layer_norm(DEEPNORM_ALPHA * x_ref[ts, :] + f, g_ref[...], b_ref[...])


def _combine_ln(y_loc, row_col, w_col, x, g, b):
    t, d = x.shape
    return pl.pallas_call(
        _combine_ln_kernel,
        grid=(t // MOE_BLOCK,),
        in_specs=[pl.BlockSpec((LOC_ROWS, d), lambda i: (i, 0)),
                  pl.BlockSpec((MOE_BLOCK, TOP_K), lambda i: (i, 0)),
                  pl.BlockSpec((MOE_BLOCK, TOP_K), lambda i: (i, 0)),
                  pl.BlockSpec((MOE_BLOCK, d), lambda i: (i, 0)),
                  pl.BlockSpec((1, d), lambda i: (0, 0)),
                  pl.BlockSpec((1, d), lambda i: (0, 0))],
        out_specs=pl.BlockSpec((MOE_BLOCK, d), lambda i: (i, 0)),
        out_shape=jax.ShapeDtypeStruct((t, d), F32),
        compiler_params=_params("parallel"),
        name="combine_ln",
    )(y_loc, row_col, w_col, x, g, b)


def _mixer(x, batch, seq, w_in, gm_ws, gm_bs, gm_ln_g, gm_ln_b, gla_wa2, gla_ba, gla_norm_g,
           w_out, ln_g, ln_b):
    w_gate = _fold_gate(w_in[:, PROJ_MAIN:], gla_wa2)
    w_ext = jnp.concatenate([w_in[:, :PROJ_MAIN], w_gate], axis=1).astype(BF16)
    del batch
    return _mixer_call(x, w_ext, w_out.astype(BF16), ln_g.reshape(1, -1), ln_b.reshape(1, -1),
                       gm_ws, gm_bs.reshape(GM_HEADS, GM_CHUNK, 1),
                       gm_ln_g.reshape(1, -1), gm_ln_b.reshape(1, -1),
                       gla_ba.reshape(1, -1), gla_norm_g.reshape(1, -1), seq)


def _granule_table(counts):
    nb = counts.shape[0]
    n_tiles = (nb * (TOP_K * MOE_BLOCK + N_EXPERTS * (GRANULE - 1))) // FFN_ROW_TILE + N_EXPERTS
    gran_per_block = LOC_ROWS // GRANULE
    g = (counts + GRANULE - 1) // GRANULE
    seg_start = jnp.cumsum(g, axis=1) - g
    cum_incl = jnp.cumsum(g, axis=0)
    cum_excl = cum_incl - g
    total = cum_incl[-1]
    tiles_per_expert = (total + GRAN_PER_TILE - 1) // GRAN_PER_TILE
    tile_end = jnp.cumsum(tiles_per_expert)
    n_used = tile_end[-1:]
    tile = jnp.arange(n_tiles, dtype=jnp.int32)
    tile_expert = jnp.minimum(jnp.sum(tile[:, None] >= tile_end[None, :], axis=1), N_EXPERTS - 1)
    pick = (tile_expert[:, None] == jnp.arange(N_EXPERTS, dtype=jnp.int32)[None, :]).astype(jnp.int32)
    per_tile = lambda tab: jnp.sum(pick[:, :, None] * tab.T[None, :, :], axis=1)
    first_tile = jnp.sum(pick * (tile_end - tiles_per_expert)[None, :], axis=1)
    vg = ((tile - first_tile)[:, None] * GRAN_PER_TILE
          + jnp.arange(GRAN_PER_TILE, dtype=jnp.int32)[None, :])[:, :, None]
    lo = per_tile(cum_excl)[:, None, :]
    hi = per_tile(cum_incl)[:, None, :]
    base = (jnp.arange(nb, dtype=jnp.int32)[None, :] * gran_per_block + per_tile(seg_start))[:, None, :]
    inside = jnp.logical_and(vg >= lo, vg < hi)
    phys = jnp.sum(jnp.where(inside, base + vg - lo, 0), axis=2)
    valid = jnp.logical_and(jnp.any(inside, axis=2), (tile < n_used[0])[:, None])
    gran = jnp.where(valid, phys, 0).astype(jnp.int32).reshape(-1)
    used = jnp.concatenate([n_used, jnp.sum(valid, axis=1)]).astype(jnp.int32)
    return tile_expert.astype(jnp.int32), used, gran


def _moe_ffn(x, w_router, e_gate, e_up, e_down, ln_g, ln_b):
    ri, rw, cnt, x_loc = _route_dispatch(x, w_router.T)
    tile_expert, n_used, gran = _granule_table(cnt[:, :, 0].astype(jnp.int32))
    y_loc = _experts(tile_expert, n_used, gran, x_loc,
                     e_gate.astype(BF16), e_up.astype(BF16), e_down.astype(BF16))
    return _combine_ln(y_loc, ri[2:2 + TOP_K].T, rw[:TOP_K].T, x,
                       ln_g.reshape(1, -1), ln_b.reshape(1, -1))


def kernel(x, w_in, gm_ws, gm_bs, gm_ln_g, gm_ln_b, gla_wa2, gla_ba, gla_norm_g, w_out, ln_mix_g, ln_mix_b, ffn_w_gate, ffn_w_up, ffn_w_down, router_w, exp_w_gate, exp_w_up, exp_w_down, ln_ffn_g, ln_ffn_b):
    batch, seq, d = x.shape
    h = x.reshape(batch * seq, d)
    for layer in range(DEPTH):
        h = _mixer(h, batch, seq, w_in[layer], gm_ws[layer], gm_bs[layer], gm_ln_g[layer],
                   gm_ln_b[layer], gla_wa2[layer], gla_ba[layer], gla_norm_g[layer], w_out[layer],
                   ln_mix_g[layer], ln_mix_b[layer])
        i = layer // 2
        if layer % 2 == 0:
            h = _swiglu_ln(h, ffn_w_gate[i].astype(BF16), ffn_w_up[i].astype(BF16),
                           ffn_w_down[i].astype(BF16), ln_ffn_g[layer].reshape(1, -1),
                           ln_ffn_b[layer].reshape(1, -1))
        else:
            h = _moe_ffn(h, router_w[i], exp_w_gate[i], exp_w_up[i], exp_w_down[i],
                         ln_ffn_g[layer], ln_ffn_b[layer])
    return h.reshape(batch, seq, d)
```

```python
import functools

import jax
import jax.numpy as jnp
from jax import lax
from jax.experimental import pallas as pl
from jax.experimental.pallas import tpu as pltpu

F32 = jnp.float32
BF16 = jnp.bfloat16

D_MODEL = 1024
DEPTH = 2
GM_HEADS = 4
GM_WIDTH = 512
GM_HEAD_DIM = 128
GM_CHUNK = 128
GLA_HEADS = 4
GLA_WIDTH = 512
GLA_DV = 128
GLA_KEY_WIDTH = 256
GLA_DK = 64
GLA_GATE_RANK = 16
GLA_TAU = 16.0
GLA_CHUNK = 64
PROJ_MAIN = 2 * GM_WIDTH + 2 * GLA_KEY_WIDTH + 2 * GLA_WIDTH
N_EXPERTS = 8
TOP_K = 2
LN_EPS = 1e-5
RMS_EPS = 1e-6
DEEPNORM_ALPHA = (2 * DEPTH) ** 0.25

V7X_VMEM_LIMIT_BYTES = 56 * 1024 * 1024
LANE = 128
SUBLANE = 8
MXU_N = 256

ROW_TILE = 512
FFN_ROW_TILE = 1024
DENSE_FF_TILE = 256
MOE_FF_TILE = 256
MOE_BLOCK = 512
GRANULE = 16
LOC_ROWS = -(-(TOP_K * MOE_BLOCK + N_EXPERTS * (GRANULE - 1)) // LANE) * LANE
COMBINE_SUB = 256
MIXER_HEAD_FILLERS = 6
MIXER_TAIL_FILLERS = 3
GRAN_UNROLL = 4
GRAN_PER_TILE = FFN_ROW_TILE // GRANULE


def _params(*sem, **kw):
    return pltpu.CompilerParams(dimension_semantics=sem, vmem_limit_bytes=V7X_VMEM_LIMIT_BYTES, **kw)


def _gelu(x):
    return 0.5 * x * (1.0 + lax.erf(x * (2.0 ** -0.5)))


def _layer_norm(y, g, b):
    mu = jnp.mean(y, axis=-1, keepdims=True)
    d = y - mu
    var = jnp.mean(d * d, axis=-1, keepdims=True)
    return d * lax.rsqrt(var + LN_EPS) * g + b


def _fold_kernel(w1_ref, w2_ref, o_ref):
    acc = jnp.zeros(o_ref.shape, F32)
    for r in range(GLA_GATE_RANK):
        acc = acc + w1_ref[:, r:r + 1] * w2_ref[r:r + 1, :]
    o_ref[...] = acc


def _fold_gate(w1, w2):
    return pl.pallas_call(
        _fold_kernel,
        out_shape=jax.ShapeDtypeStruct((w1.shape[0], w2.shape[1]), F32),
        name="fold_gate",
    )(w1, w2)


def _inproj_items(x_ref, w_ref, p_ref, z_ref):
    cache = {}

    def xb():
        if "xb" not in cache:
            cache["xb"] = x_ref[...].astype(BF16)
        return cache["xb"]

    def block(n):
        cols = slice(n * MXU_N, (n + 1) * MXU_N)

        def run():
            p_ref[:, cols] = jnp.dot(xb(), w_ref[:, cols], preferred_element_type=F32).astype(BF16)
        return run

    def gate_logits():
        z_ref[...] = jnp.dot(xb(), w_ref[:, PROJ_MAIN:], preferred_element_type=F32)

    return [block(n) for n in range(PROJ_MAIN // MXU_N)] + [gate_logits]


def _gmlp_items(uv_ref, w_ref, bcol_ref, g_ref, b_ref, o_ref):
    rows = uv_ref.shape[0]

    def head(h):
        cols = slice(h * GM_HEAD_DIM, (h + 1) * GM_HEAD_DIM)
        vcols = slice(GM_WIDTH + h * GM_HEAD_DIM, GM_WIDTH + (h + 1) * GM_HEAD_DIM)

        def run():
            r = lax.broadcasted_iota(jnp.int32, (GM_CHUNK, GM_CHUNK), 0)
            c = lax.broadcasted_iota(jnp.int32, (GM_CHUNK, GM_CHUNK), 1)
            wm = jnp.where(r >= c, w_ref[h], 0.0).astype(BF16)
            v = _gelu(uv_ref[:, vcols].astype(F32))
            vn = _layer_norm(v, g_ref[:, cols], b_ref[:, cols]).astype(BF16)
            u = _gelu(uv_ref[:, cols].astype(F32))
            for ci in range(rows // GM_CHUNK):
                rs = slice(ci * GM_CHUNK, (ci + 1) * GM_CHUNK)
                s = jnp.dot(wm, vn[rs], preferred_element_type=F32) + bcol_ref[h]
                o_ref[rs, cols] = (u[rs] * s).astype(BF16)
        return run

    return [head(h) for h in range(GM_HEADS)]


def _gla_items(q_ref, k_ref, v_ref, g_ref, z_ref, ba_ref, ng_ref, o_ref,
               st_ref, qd_ref, ki_ref, kd_ref, dec_ref, raw_ref):
    rows = q_ref.shape[0]
    n_chunks = rows // GLA_CHUNK
    shift = GLA_CHUNK.bit_length() - 1
    nt = (((1,), (1,)), ((), ()))
    tn = (((0,), (0,)), ((), ()))

    def prep():
        r = lax.broadcasted_iota(jnp.int32, (rows, rows), 0)
        c = lax.broadcasted_iota(jnp.int32, (rows, rows), 1)
        ones_tril = jnp.where(jnp.logical_and((r >> shift) == (c >> shift), r >= c),
                              1.0, 0.0).astype(BF16)
        z = z_ref[...] + ba_ref[...]
        la = (jnp.minimum(z, 0.0) - jnp.log1p(jnp.exp(-jnp.abs(z)))) * (1.0 / GLA_TAU)
        la_hi = la.astype(BF16)
        la_lo = (la - la_hi.astype(F32)).astype(BF16)
        b = (jnp.dot(ones_tril, la_hi, preferred_element_type=F32)
             + jnp.dot(ones_tril, la_lo, preferred_element_type=F32))
        b3 = b.reshape(n_chunks, GLA_CHUNK, GLA_KEY_WIDTH)
        b_last = b3[:, GLA_CHUNK - 1:GLA_CHUNK, :]
        k = k_ref[...].astype(F32)
        qd_ref[...] = (q_ref[...].astype(F32) * (GLA_DK ** -0.5) * jnp.exp(b)).astype(BF16)
        ki_ref[...] = (k * jnp.exp(-b)).astype(BF16)
        kd_ref[...] = (k * jnp.exp(b_last - b3).reshape(rows, GLA_KEY_WIDTH)).astype(BF16)
        dec_ref[...] = jnp.exp(b_last).reshape(n_chunks, GLA_KEY_WIDTH)

    def chunk(ci):
        rs = slice(ci * GLA_CHUNK, (ci + 1) * GLA_CHUNK)

        def run(between=None):
            rc = lax.broadcasted_iota(jnp.int32, (GLA_CHUNK, GLA_CHUNK), 0)
            cc = lax.broadcasted_iota(jnp.int32, (GLA_CHUNK, GLA_CHUNK), 1)
            causal = rc >= cc
            q_dec = qd_ref[rs, :]
            k_inv = ki_ref[rs, :]
            k_dec = kd_ref[rs, :]
            decay = dec_ref[ci:ci + 1, :]
            head_cols = [slice(h * GLA_DK, (h + 1) * GLA_DK) for h in range(GLA_HEADS)]
            all_scores = [lax.dot_general(q_dec[:, ks], k_inv[:, ks], nt, preferred_element_type=F32)
                          for ks in head_cols]
            if between is not None:
                between()
            for h in range(GLA_HEADS):
                ks = head_cols[h]
                vs = slice(h * GLA_DV, (h + 1) * GLA_DV)
                scores = jnp.where(causal, all_scores[h], 0.0).astype(BF16)
                vh = v_ref[rs, vs]
                state_t = st_ref[h]
                o = (jnp.dot(scores, vh, preferred_element_type=F32)
                     + lax.dot_general(q_dec[:, ks], state_t.astype(BF16), nt,
                                       preferred_element_type=F32))
                st_ref[h] = (state_t * decay[:, ks]
                             + lax.dot_general(vh, k_dec[:, ks], tn, preferred_element_type=F32))
                raw_ref[rs, vs] = o
        return run

    def finish():
        for h in range(GLA_HEADS):
            vs = slice(h * GLA_DV, (h + 1) * GLA_DV)
            o = raw_ref[:, vs]
            o = o * lax.rsqrt(jnp.mean(o * o, axis=-1, keepdims=True) + RMS_EPS) * ng_ref[:, vs]
            gate = g_ref[:, vs].astype(F32)
            o_ref[:, vs] = (o * (gate * jax.nn.sigmoid(gate))).astype(BF16)

    return prep, [chunk(ci) for ci in range(n_chunks)], finish


def _mixer_kernel(xa_ref, xc_ref, win_ref, wout_ref, lng_ref, lnb_ref,
                  gw_ref, gbcol_ref, gg_ref, gb_ref, ba_ref, ng_ref, o_ref,
                  proj0, proj1, z0, z1, ab0, ab1, st_ref, *gla_scratch, tiles_per_seq):
    t = pl.program_id(0)
    proj_scr, z_scr, ab_scr = (proj0, proj1), (z0, z1), (ab0, ab1)

    @pl.when(t == 0)
    def _():
        for ref in (proj0, proj1, z0, z1, ab0, ab1, st_ref):
            ref[...] = jnp.zeros(ref.shape, ref.dtype)

    @pl.when(lax.rem(t + tiles_per_seq - 1, tiles_per_seq) == 0)
    def _():
        st_ref[...] = jnp.zeros(st_ref.shape, F32)

    q0 = 2 * GM_WIDTH
    k0 = q0 + GLA_KEY_WIDTH
    v0 = k0 + GLA_KEY_WIDTH
    g0 = v0 + GLA_WIDTH

    def stages(proj_a, z_a, proj_b, z_b, ab_b, ab_c):
        inproj = _inproj_items(xa_ref, win_ref, proj_a, z_a)
        gla_prep, gla_chunks, gla_finish = _gla_items(
            proj_b.at[:, q0:k0], proj_b.at[:, k0:v0], proj_b.at[:, v0:g0],
            proj_b.at[:, g0:PROJ_MAIN], z_b, ba_ref, ng_ref, ab_b.at[:, GM_WIDTH:],
            st_ref, *gla_scratch)
        gmlp = _gmlp_items(proj_b.at[:, :q0], gw_ref, gbcol_ref, gg_ref, gb_ref,
                           ab_b.at[:, :GM_WIDTH])
        outproj, outproj_finish = _outproj_items(ab_c.at[:, :GM_WIDTH], ab_c.at[:, GM_WIDTH:],
                                                 wout_ref, xc_ref, lng_ref, lnb_ref, o_ref)
        fillers = inproj + outproj
        head, tail = MIXER_HEAD_FILLERS, len(fillers) - MIXER_TAIL_FILLERS
        for run in fillers[:head // 2]:
            run()
        gla_prep()
        for run in fillers[head // 2:head]:
            run()
        done = head
        for i, run_chunk in enumerate(gla_chunks):
            target = head + (i + 1) * (tail - head) // len(gla_chunks)
            mid, done = fillers[done:target], target
            run_chunk(between=lambda mid=mid: [run() for run in mid])
        late = fillers[tail:] + [gla_finish]
        for i, run_head in enumerate(gmlp):
            for run in late[i * len(late) // len(gmlp):(i + 1) * len(late) // len(gmlp)]:
                run()
            run_head()
        outproj_finish()

    for parity in range(2):
        @pl.when(t % 2 == parity)
        def _(a=parity, b=1 - parity):
            stages(proj_scr[a], z_scr[a], proj_scr[b], z_scr[b], ab_scr[b], ab_scr[a])


def _mixer_call(x, w_ext, w_out, ln_g, ln_b, gm_ws, gm_bcol, gm_ln_g, gm_ln_b, ba, norm_g, seq):
    t, d = x.shape
    n_tiles = t // ROW_TILE
    kw, vw = GLA_KEY_WIDTH, GLA_WIDTH
    const2 = lambda i: (0, 0)
    const3 = lambda i: (0, 0, 0)
    resident = pl.Buffered(1)
    tile_c = lambda i: (jnp.maximum(i - 2, 0), 0)
    return pl.pallas_call(
        functools.partial(_mixer_kernel, tiles_per_seq=seq // ROW_TILE),
        grid=(n_tiles + 2,),
        in_specs=[pl.BlockSpec((ROW_TILE, d), lambda i: (jnp.minimum(i, n_tiles - 1), 0)),
                  pl.BlockSpec((ROW_TILE, d), tile_c),
                  pl.BlockSpec(w_ext.shape, const2, pipeline_mode=resident),
                  pl.BlockSpec(w_out.shape, const2, pipeline_mode=resident),
                  pl.BlockSpec((1, d), const2),
                  pl.BlockSpec((1, d), const2),
                  pl.BlockSpec((GM_HEADS, GM_CHUNK, GM_CHUNK), const3),
                  pl.BlockSpec((GM_HEADS, GM_CHUNK, 1), const3),
                  pl.BlockSpec((1, GM_WIDTH), const2),
                  pl.BlockSpec((1, GM_WIDTH), const2),
                  pl.BlockSpec((1, kw), const2),
                  pl.BlockSpec((1, vw), const2)],
        out_specs=pl.BlockSpec((ROW_TILE, d), tile_c),
        out_shape=jax.ShapeDtypeStruct((t, d), F32),
        scratch_shapes=[pltpu.VMEM((ROW_TILE, PROJ_MAIN), BF16),
                        pltpu.VMEM((ROW_TILE, PROJ_MAIN), BF16),
                        pltpu.VMEM((ROW_TILE, kw), F32),
                        pltpu.VMEM((ROW_TILE, kw), F32),
                        pltpu.VMEM((ROW_TILE, GM_WIDTH + vw), BF16),
                        pltpu.VMEM((ROW_TILE, GM_WIDTH + vw), BF16),
                        pltpu.VMEM((GLA_HEADS, GLA_DV, GLA_DK), F32),
                        pltpu.VMEM((ROW_TILE, kw), BF16),
                        pltpu.VMEM((ROW_TILE, kw), BF16),
                        pltpu.VMEM((ROW_TILE, kw), BF16),
                        pltpu.VMEM((ROW_TILE // GLA_CHUNK, kw), F32),
                        pltpu.VMEM((ROW_TILE, vw), F32)],
        compiler_params=_params("arbitrary"),
        name="mixer",
    )(x, x, w_ext, w_out, ln_g, ln_b, gm_ws, gm_bcol, gm_ln_g, gm_ln_b, ba, norm_g)


def _outproj_items(a_ref, b_ref, w_ref, x_ref, g_ref, be_ref, o_ref):
    def block(n):
        cols = slice(n * MXU_N, (n + 1) * MXU_N)

        def run():
            h = (jnp.dot(a_ref[...], w_ref[:GM_WIDTH, cols], preferred_element_type=F32)
                 + jnp.dot(b_ref[...], w_ref[GM_WIDTH:, cols], preferred_element_type=F32))
            o_ref[:, cols] = DEEPNORM_ALPHA * x_ref[:, cols] + h
        return run

    def finish():
        o_ref[...] = _layer_norm(o_ref[...], g_ref[...], be_ref[...])

    return [block(n) for n in range(o_ref.shape[1] // MXU_N)], finish


def _swiglu_ln_kernel(x_ref, wg_ref, wu_ref, wd_ref, g_ref, b_ref, o_ref, hid_ref):
    xb = x_ref[...].astype(BF16)
    for n in range(wg_ref.shape[1] // DENSE_FF_TILE):
        cols = slice(n * DENSE_FF_TILE, (n + 1) * DENSE_FF_TILE)
        gate = jnp.dot(xb, wg_ref[:, cols], preferred_element_type=F32)
        up = jnp.dot(xb, wu_ref[:, cols], preferred_element_type=F32)
        hid_ref[:, cols] = (gate * jax.nn.sigmoid(gate) * up).astype(BF16)
    f = jnp.dot(hid_ref[...], wd_ref[...], preferred_element_type=F32)
    o_ref[...] = _layer_norm(DEEPNORM_ALPHA * x_ref[...] + f, g_ref[...], b_ref[...])


def _swiglu_ln(x, wg, wu, wd, g, b):
    rows, d = x.shape
    ff = wg.shape[1]
    resident = pl.Buffered(1)
    return pl.pallas_call(
        _swiglu_ln_kernel,
        grid=(rows // FFN_ROW_TILE,),
        in_specs=[pl.BlockSpec((FFN_ROW_TILE, d), lambda i: (i, 0)),
                  pl.BlockSpec((d, ff), lambda i: (0, 0), pipeline_mode=resident),
                  pl.BlockSpec((d, ff), lambda i: (0, 0), pipeline_mode=resident),
                  pl.BlockSpec((ff, d), lambda i: (0, 0), pipeline_mode=resident),
                  pl.BlockSpec((1, d), lambda i: (0, 0)),
                  pl.BlockSpec((1, d), lambda i: (0, 0))],
        out_specs=pl.BlockSpec((FFN_ROW_TILE, d), lambda i: (i, 0)),
        out_shape=jax.ShapeDtypeStruct((rows, d), F32),
        scratch_shapes=[pltpu.VMEM((FFN_ROW_TILE, ff), BF16)],
        compiler_params=_params("parallel"),
        name="swiglu_ln",
    )(x, wg, wu, wd, g, b)


def _split_bf16(v):
    hi = v.astype(BF16)
    return hi, (v - hi.astype(F32)).astype(BF16)


def _router_kernel(x_ref, wr_ref, ri_ref, rw_ref, cnt_ref):
    tb = x_ref.shape[0]
    x_hi, x_lo = _split_bf16(x_ref[...])
    w_hi, w_lo = _split_bf16(wr_ref[...])
    nt = (((1,), (1,)), ((), ()))
    logits = (lax.dot_general(w_hi, x_hi, nt, preferred_element_type=F32)
              + lax.dot_general(w_hi, x_lo, nt, preferred_element_type=F32)
              + lax.dot_general(w_lo, x_hi, nt, preferred_element_type=F32))
    e_iota = lax.broadcasted_iota(jnp.int32, (N_EXPERTS, tb), 0)
    m0 = jnp.max(logits, axis=0, keepdims=True)
    i0 = jnp.min(jnp.where(logits == m0, e_iota, N_EXPERTS), axis=0, keepdims=True)
    rest = jnp.where(e_iota == i0, -jnp.inf, logits)
    m1 = jnp.max(rest, axis=0, keepdims=True)
    i1 = jnp.min(jnp.where(rest == m1, e_iota, N_EXPERTS), axis=0, keepdims=True)
    ex = jnp.exp(m1 - m0)
    w0 = 1.0 / (1.0 + ex)
    w1 = ex / (1.0 + ex)
    oh0 = e_iota == i0
    oh1 = e_iota == i1
    onehot = jnp.where(jnp.logical_or(oh0, oh1), 1.0, 0.0)
    tr = lax.broadcasted_iota(jnp.int32, (tb, tb), 0)
    tc = lax.broadcasted_iota(jnp.int32, (tb, tb), 1)
    before = jnp.where(tr < tc, 1.0, 0.0).astype(BF16)
    rank = jnp.dot(onehot.astype(BF16), before, preferred_element_type=F32)
    count = jnp.broadcast_to(jnp.sum(onehot, axis=1, keepdims=True), (N_EXPERTS, LANE))
    cap = jnp.ceil(count * (1.0 / GRANULE)) * GRANULE
    e_sub = lax.broadcasted_iota(jnp.int32, (N_EXPERTS, LANE), 0)
    seg = jnp.zeros((N_EXPERTS, LANE), F32)
    for e in range(N_EXPERTS - 1):
        seg = seg + jnp.where(e_sub > e, cap[e:e + 1, :], 0.0)
    row = rank + seg[:, 0:1]
    row0 = jnp.sum(jnp.where(oh0, row, 0.0), axis=0, keepdims=True)
    row1 = jnp.sum(jnp.where(oh1, row, 0.0), axis=0, keepdims=True)
    cnt_ref[...] = count
    ri_ref[...] = jnp.zeros(ri_ref.shape, jnp.int32)
    ri_ref[0:1, :] = i0
    ri_ref[1:2, :] = i1
    ri_ref[2:3, :] = row0.astype(jnp.int32)
    ri_ref[3:4, :] = row1.astype(jnp.int32)
    rw_ref[...] = jnp.zeros(rw_ref.shape, F32)
    rw_ref[0:1, :] = w0
    rw_ref[1:2, :] = w1
    return x_hi


def _route_dispatch_kernel(x_ref, wr_ref, ri_ref, rw_ref, cnt_ref, o_ref):
    xb = _router_kernel(x_ref, wr_ref, ri_ref, rw_ref, cnt_ref)
    rows, tb = o_ref.shape[0], x_ref.shape[0]
    r = lax.broadcasted_iota(jnp.int32, (rows, tb), 0)
    sel = jnp.logical_or(r == ri_ref[2:3, :], r == ri_ref[3:4, :])
    perm = jnp.where(sel, 1.0, 0.0).astype(BF16)
    o_ref[...] = jnp.dot(perm, xb, preferred_element_type=F32).astype(BF16)


def _route_dispatch(x, wr_t):
    t, d = x.shape
    nb = t // MOE_BLOCK
    return pl.pallas_call(
        _route_dispatch_kernel,
        grid=(nb,),
        in_specs=[pl.BlockSpec((MOE_BLOCK, d), lambda i: (i, 0)),
                  pl.BlockSpec((N_EXPERTS, d), lambda i: (0, 0))],
        out_specs=[pl.BlockSpec((SUBLANE, MOE_BLOCK), lambda i: (0, i)),
                   pl.BlockSpec((SUBLANE, MOE_BLOCK), lambda i: (0, i)),
                   pl.BlockSpec((None, N_EXPERTS, LANE), lambda i: (i, 0, 0)),
                   pl.BlockSpec((LOC_ROWS, d), lambda i: (i, 0))],
        out_shape=[jax.ShapeDtypeStruct((SUBLANE, t), jnp.int32),
                   jax.ShapeDtypeStruct((SUBLANE, t), F32),
                   jax.ShapeDtypeStruct((nb, N_EXPERTS, LANE), F32),
                   jax.ShapeDtypeStruct((nb * LOC_ROWS, d), BF16)],
        compiler_params=_params("parallel"),
        name="route_dispatch",
    )(x, wr_t)


def _experts_kernel(te_ref, nu_ref, gran_ref, x_hbm, wg_hbm, wu_hbm, wd_hbm, y_hbm,
                    xbuf, ybuf, hid_ref, wg_buf, wu_buf, wd_buf, sem_in, sem_out, sem_w):
    i = pl.program_id(0)
    n_tiles = pl.num_programs(0)
    n_used = nu_ref[0]
    n_chunks = wg_buf.shape[0]
    expert = te_ref[i]
    prev_expert = te_ref[jnp.maximum(i - 1, 0)]
    next_expert = te_ref[jnp.minimum(i + 1, n_tiles - 1)]
    first_of_expert = jnp.logical_or(i == 0, prev_expert != expert)
    refill = jnp.logical_and(i + 1 < n_used, next_expert != expert)

    def up_copies(e, n):
        cols = pl.ds(n * MOE_FF_TILE, MOE_FF_TILE)
        return (pltpu.make_async_copy(wg_hbm.at[e, :, cols], wg_buf.at[n], sem_w.at[n]),
                pltpu.make_async_copy(wu_hbm.at[e, :, cols], wu_buf.at[n], sem_w.at[n]))

    def down_copy(e):
        return pltpu.make_async_copy(wd_hbm.at[e], wd_buf, sem_w.at[n_chunks])

    def for_granules(tile, fn):
        def one(s):
            g = gran_ref[tile * GRAN_PER_TILE + s]
            fn(pl.ds(pl.multiple_of(g * GRANULE, GRANULE), GRANULE),
               pl.ds(pl.multiple_of(s * GRANULE, GRANULE), GRANULE))

        def group(p, carry):
            for u in range(GRAN_UNROLL):
                one(p * GRAN_UNROLL + u)
            return carry

        def single(s, carry):
            one(s)
            return carry

        filled = nu_ref[1 + tile]
        n_groups = filled // GRAN_UNROLL
        lax.fori_loop(0, n_groups, group, 0)
        lax.fori_loop(n_groups * GRAN_UNROLL, filled, single, 0)

    def gather(slot):
        return lambda hbm_rows, tile_rows: pltpu.make_async_copy(
            x_hbm.at[hbm_rows], xbuf.at[slot, tile_rows], sem_in.at[slot])

    def scatter(hbm_rows, tile_rows):
        return pltpu.make_async_copy(ybuf.at[tile_rows], y_hbm.at[hbm_rows], sem_out.at[0])

    @pl.when(i < n_used)
    def _():
        slot = i % 2

        @pl.when(i == 0)
        def _():
            for n in range(n_chunks):
                for cp in up_copies(expert, n):
                    cp.start()
            down_copy(expert).start()
            xbuf[...] = jnp.zeros(xbuf.shape, BF16)
            for_granules(0, lambda h, r: gather(0)(h, r).start())

        for_granules(i, lambda h, r: gather(slot)(h, r).wait())

        @pl.when(i + 1 < n_used)
        def _():
            for_granules(i + 1, lambda h, r: gather(1 - slot)(h, r).start())

        @pl.when(first_of_expert)
        def _():
            for n in range(n_chunks):
                for cp in up_copies(expert, n):
                    cp.wait()

        xb = xbuf[slot]
        for n in range(n_chunks):
            gate = jnp.dot(xb, wg_buf[n], preferred_element_type=F32)
            up = jnp.dot(xb, wu_buf[n], preferred_element_type=F32)
            hid_ref[:, n * MOE_FF_TILE:(n + 1) * MOE_FF_TILE] = (
                gate * jax.nn.sigmoid(gate) * up).astype(BF16)

        @pl.when(refill)
        def _():
            for n in range(n_chunks):
                for cp in up_copies(next_expert, n):
                    cp.start()

        @pl.when(first_of_expert)
        def _():
            down_copy(expert).wait()

        @pl.when(i > 0)
        def _():
            for_granules(i - 1, lambda h, r: scatter(h, r).wait())

        ybuf[...] = jnp.dot(hid_ref[...], wd_buf[...], preferred_element_type=F32).astype(BF16)

        @pl.when(refill)
        def _():
            down_copy(next_expert).start()

        for_granules(i, lambda h, r: scatter(h, r).start())

        @pl.when(i == n_used - 1)
        def _():
            for_granules(i, lambda h, r: scatter(h, r).wait())


def _experts(tile_expert, used, gran, x_loc, wg, wu, wd):
    rows, d = x_loc.shape
    ff = wg.shape[2]
    n_tiles = tile_expert.shape[0]
    nj = ff // MOE_FF_TILE
    hbm = pl.BlockSpec(memory_space=pl.ANY)
    return pl.pallas_call(
        _experts_kernel,
        grid_spec=pltpu.PrefetchScalarGridSpec(
            num_scalar_prefetch=3,
            grid=(n_tiles,),
            in_specs=[hbm, hbm, hbm, hbm],
            out_specs=hbm,
            scratch_shapes=[pltpu.VMEM((2, FFN_ROW_TILE, d), BF16),
                            pltpu.VMEM((FFN_ROW_TILE, d), BF16),
                            pltpu.VMEM((FFN_ROW_TILE, ff), BF16),
                            pltpu.VMEM((nj, d, MOE_FF_TILE), BF16),
                            pltpu.VMEM((nj, d, MOE_FF_TILE), BF16),
                            pltpu.VMEM((ff, d), BF16),
                            pltpu.SemaphoreType.DMA((2,)),
                            pltpu.SemaphoreType.DMA((1,)),
                            pltpu.SemaphoreType.DMA((nj + 1,))]),
        out_shape=jax.ShapeDtypeStruct((rows, d), BF16),
        input_output_aliases={3: 0},
        compiler_params=_params("arbitrary"),
        name="swiglu_experts",
    )(tile_expert, used, gran, x_loc, wg, wu, wd)


def _combine_ln_kernel(y_ref, row_ref, w_ref, x_ref, g_ref, b_ref, o_ref):
    tb, rows = x_ref.shape[0], y_ref.shape[0]
    for k in range(tb // COMBINE_SUB):
        ts = slice(k * COMBINE_SUB, (k + 1) * COMBINE_SUB)
        c = lax.broadcasted_iota(jnp.int32, (COMBINE_SUB, rows), 1)
        q = (jnp.where(c == row_ref[ts, 0:1], w_ref[ts, 0:1], 0.0)
             + jnp.where(c == row_ref[ts, 1:2], w_ref[ts, 1:2], 0.0))
        f = jnp.dot(q.astype(BF16), y_ref[...], preferred_element_type=F32)
        o_ref[ts, :] = _layer_norm(DEEPNORM_ALPHA * x_ref[ts, :] + f, g_ref[...], b_ref[...])


def _combine_ln(y_loc, row_col, w_col, x, g, b):
    t, d = x.shape
    return pl.pallas_call(
        _combine_ln_kernel,
        grid=(t // MOE_BLOCK,),
        in_specs=[pl.BlockSpec((LOC_ROWS, d), lambda i: (i, 0)),
                  pl.BlockSpec((MOE_BLOCK, TOP_K), lambda i: (i, 0)),
                  pl.BlockSpec((MOE_BLOCK, TOP_K), lambda i: (i, 0)),
                  pl.BlockSpec((MOE_BLOCK, d), lambda i: (i, 0)),
                  pl.BlockSpec((1, d), lambda i: (0, 0)),
                  pl.BlockSpec((1, d), lambda i: (0, 0))],
        out_specs=pl.BlockSpec((MOE_BLOCK, d), lambda i: (i, 0)),
        out_shape=jax.ShapeDtypeStruct((t, d), F32),
        compiler_params=_params("parallel"),
        name="combine_ln",
    )(y_loc, row_col, w_col, x, g, b)


def _mixer(x, batch, seq, w_in, gm_ws, gm_bs, gm_ln_g, gm_ln_b, gla_wa2, gla_ba, gla_norm_g,
           w_out, ln_g, ln_b):
    w_gate = _fold_gate(w_in[:, PROJ_MAIN:], gla_wa2)
    w_ext = jnp.concatenate([w_in[:, :PROJ_MAIN], w_gate], axis=1).astype(BF16)
    del batch
    return _mixer_call(x, w_ext, w_out.astype(BF16), ln_g.reshape(1, -1), ln_b.reshape(1, -1),
                       gm_ws, gm_bs.reshape(GM_HEADS, GM_CHUNK, 1),
                       gm_ln_g.reshape(1, -1), gm_ln_b.reshape(1, -1),
                       gla_ba.reshape(1, -1), gla_norm_g.reshape(1, -1), seq)


def _granule_table(counts):
    nb = counts.shape[0]
    n_tiles = (nb * (TOP_K * MOE_BLOCK + N_EXPERTS * (GRANULE - 1))) // FFN_ROW_TILE + N_EXPERTS
    gran_per_block = LOC_ROWS // GRANULE
    g = (counts + GRANULE - 1) // GRANULE
    seg_start = jnp.cumsum(g, axis=1) - g
    cum_incl = jnp.cumsum(g, axis=0)
    cum_excl = cum_incl - g
    total = cum_incl[-1]
    tiles_per_expert = (total + GRAN_PER_TILE - 1) // GRAN_PER_TILE
    tile_end = jnp.cumsum(tiles_per_expert)
    n_used = tile_end[-1:]
    tile = jnp.arange(n_tiles, dtype=jnp.int32)
    tile_expert = jnp.minimum(jnp.sum(tile[:, None] >= tile_end[None, :], axis=1), N_EXPERTS - 1)
    pick = (tile_expert[:, None] == jnp.arange(N_EXPERTS, dtype=jnp.int32)[None, :]).astype(jnp.int32)
    per_tile = lambda tab: jnp.sum(pick[:, :, None] * tab.T[None, :, :], axis=1)
    first_tile = jnp.sum(pick * (tile_end - tiles_per_expert)[None, :], axis=1)
    vg = ((tile - first_tile)[:, None] * GRAN_PER_TILE
          + jnp.arange(GRAN_PER_TILE, dtype=jnp.int32)[None, :])[:, :, None]
    lo = per_tile(cum_excl)[:, None, :]
    hi = per_tile(cum_incl)[:, None, :]
    base = (jnp.arange(nb, dtype=jnp.int32)[None, :] * gran_per_block + per_tile(seg_start))[:, None, :]
    inside = jnp.logical_and(vg >= lo, vg < hi)
    phys = jnp.sum(jnp.where(inside, base + vg - lo, 0), axis=2)
    valid = jnp.logical_and(jnp.any(inside, axis=2), (tile < n_used[0])[:, None])
    gran = jnp.where(valid, phys, 0).astype(jnp.int32).reshape(-1)
    used = jnp.concatenate([n_used, jnp.sum(valid, axis=1)]).astype(jnp.int32)
    return tile_expert.astype(jnp.int32), used, gran


def _moe_ffn(x, w_router, e_gate, e_up, e_down, ln_g, ln_b):
    ri, rw, cnt, x_loc = _route_dispatch(x, w_router.T)
    tile_expert, n_used, gran = _granule_table(cnt[:, :, 0].astype(jnp.int32))
    y_loc = _experts(tile_expert, n_used, gran, x_loc,
                     e_gate.astype(BF16), e_up.astype(BF16), e_down.astype(BF16))
    return _combine_ln(y_loc, ri[2:2 + TOP_K].T, rw[:TOP_K].T, x,
                       ln_g.reshape(1, -1), ln_b.reshape(1, -1))


def kernel(x, w_in, gm_ws, gm_bs, gm_ln_g, gm_ln_b, gla_wa2, gla_ba, gla_norm_g, w_out, ln_mix_g, ln_mix_b, ffn_w_gate, ffn_w_up, ffn_w_down, router_w, exp_w_gate, exp_w_up, exp_w_down, ln_ffn_g, ln_ffn_b):
    batch, seq, d = x.shape
    h = x.reshape(batch * seq, d)
    for layer in range(DEPTH):
        h = _mixer(h, batch, seq, w_in[layer], gm_ws[layer], gm_bs[layer], gm_ln_g[layer],
                   gm_ln_b[layer], gla_wa2[layer], gla_ba[layer], gla_norm_g[layer], w_out[layer],
                   ln_mix_g[layer], ln_mix_b[layer])
        i = layer // 2
        if layer % 2 == 0:
            h = _swiglu_ln(h, ffn_w_gate[i].astype(BF16), ffn_w_up[i].astype(BF16),
                           ffn_w_down[i].astype(BF16), ln_ffn_g[layer].reshape(1, -1),
                           ln_ffn_b[layer].reshape(1, -1))
        else:
            h = _moe_ffn(h, router_w[i], exp_w_gate[i], exp_w_up[i], exp_w_down[i],
                         ln_ffn_g[layer], ln_ffn_b[layer])
    return h.reshape(batch, seq, d)
```
